```python
import math, functools
import jax, jax.numpy as jnp
from jax import lax
import numpy as np

D_MODEL = 1024
BATCH = 16
SEQ = 2048
DEPTH = 2

CHUNK = 64
Q_BLOCK = 128
EPS = 1e-6
MAX_START = 4096

MLA_HEADS = 4
MLA_Q_LORA = 384
MLA_KV_LORA = 256
MLA_NOPE = 128
MLA_ROPE = 64
MLA_V = 128
ROPE_THETA = 10000.0

RWKV_HEADS = 8
RWKV_HEAD = 64
RWKV_WIDTH = RWKV_HEADS * RWKV_HEAD
RWKV_DECAY_LORA = 64
RWKV_A_LORA = 64
RWKV_GATE_LORA = 160
RWKV_GN_EPS = 64e-5

GDN_HEADS = 4
GDN_HEAD = 128
GDN_WIDTH = GDN_HEADS * GDN_HEAD
CONV_K = 4

SSD_HEADS = 8
SSD_HEAD = 64
SSD_WIDTH = SSD_HEADS * SSD_HEAD
SSD_GROUPS = 2
SSD_STATE = 128
SSD_CONV_DIM = SSD_WIDTH + 2 * SSD_GROUPS * SSD_STATE
DT_MIN = 0.001
DT_MAX = 0.1

N_BRANCH = 4
BRANCH_WIDTH = 512

FF_DENSE = 2816
N_EXPERTS = 8
TOP_K = 2
FF_EXPERT = 3584
MOE_BLOCK = 512

MLA_COLS = MLA_Q_LORA + MLA_KV_LORA + MLA_ROPE
RWKV_COLS = 3 * RWKV_WIDTH + RWKV_DECAY_LORA + RWKV_A_LORA + RWKV_GATE_LORA
GDN_COLS = 4 * GDN_WIDTH + 2 * GDN_HEADS
SSD_COLS = SSD_WIDTH + SSD_CONV_DIM + SSD_HEADS
GATE_COLS = N_BRANCH * D_MODEL
OFF_RWKV = MLA_COLS
OFF_GDN = OFF_RWKV + RWKV_COLS
OFF_SSD = OFF_GDN + GDN_COLS
OFF_GATE = OFF_SSD + SSD_COLS
IN_COLS = OFF_GATE + GATE_COLS

kernel_name = "hybrid_mla_rwkv7_gdn_ssd_moe_trunk"


def rms_norm(x, g, eps=EPS):
    xf = x.astype(jnp.float32)
    y = xf * lax.rsqrt(jnp.mean(xf * xf, axis=-1, keepdims=True) + eps)
    return (y * g.astype(jnp.float32)).astype(x.dtype)


def rope(x, positions):
    half = x.shape[-1] // 2
    inv_freq = ROPE_THETA ** (-jnp.arange(half, dtype=jnp.float32) / half)
    ang = positions.astype(jnp.float32)[..., None] * inv_freq
    ang = ang.reshape(ang.shape[:2] + (1,) * (x.ndim - 3) + (half,))
    cos, sin = jnp.cos(ang), jnp.sin(ang)
    x1 = x[..., :half].astype(jnp.float32)
    x2 = x[..., half:].astype(jnp.float32)
    return jnp.concatenate([x1 * cos - x2 * sin, x2 * cos + x1 * sin], axis=-1).astype(x.dtype)


def causal_dwconv(x, w):
    k, c = w.shape
    return lax.conv_general_dilated(
        x, w.astype(x.dtype)[:, None, :], window_strides=(1,), padding=[(k - 1, 0)],
        dimension_numbers=("NWC", "WIO", "NWC"), feature_group_count=c)


def segsum_exp(a):
    cs = jnp.cumsum(a, axis=-1)
    t = a.shape[-1]
    mask = jnp.tril(jnp.ones((t, t), bool))
    return jnp.exp(jnp.where(mask, cs[..., :, None] - cs[..., None, :], -jnp.inf))


def l2norm(x):
    return x * lax.rsqrt(jnp.sum(x * x, axis=-1, keepdims=True) + EPS)


def mla_attention(p, positions, q_norm, kv_norm, w_uq, w_ukv):
    B, S, _ = p.shape
    H = MLA_HEADS
    c_q = p[..., :MLA_Q_LORA]
    c_kv = p[..., MLA_Q_LORA:MLA_Q_LORA + MLA_KV_LORA]
    k_rope = rope(p[..., MLA_Q_LORA + MLA_KV_LORA:], positions)
    q = (rms_norm(c_q, q_norm) @ w_uq).reshape(B, S, H, MLA_NOPE + MLA_ROPE)
    q_nope, q_rope = q[..., :MLA_NOPE], rope(q[..., MLA_NOPE:], positions)
    kv = (rms_norm(c_kv, kv_norm) @ w_ukv).reshape(B, S, H, MLA_NOPE + MLA_V)
    k_nope, v = kv[..., :MLA_NOPE], kv[..., MLA_NOPE:]
    scale = (MLA_NOPE + MLA_ROPE) ** -0.5
    chunk_id = jnp.arange(S) // CHUNK
    outs = []
    for blk in range(S // Q_BLOCK):
        q0, q1 = blk * Q_BLOCK, (blk + 1) * Q_BLOCK
        s = (jnp.einsum("bqhd,bkhd->bhqk", q_nope[:, q0:q1], k_nope[:, :q1])
             + jnp.einsum("bqhr,bkr->bhqk", q_rope[:, q0:q1], k_rope[:, :q1]))
        s = s.astype(jnp.float32) * scale
        allowed = chunk_id[None, :q1] <= chunk_id[q0:q1, None]
        prob = jax.nn.softmax(jnp.where(allowed, s, -jnp.inf), axis=-1).astype(v.dtype)
        outs.append(jnp.einsum("bhqk,bkhd->bqhd", prob, v[:, :q1]))
    return jnp.concatenate(outs, axis=1).reshape(B, S, H * MLA_V)


def rwkv7_time_mix(p, mu, w0, w2, a0, a2, g2, k_k, k_a, r_k, ln_w, ln_b):
    B, S, _ = p.shape
    H, N = RWKV_HEADS, RWKV_HEAD
    f32 = jnp.float32
    prev = jnp.pad(p, ((0, 0), (1, 0), (0, 0)))[:, :-1]
    p = p + (prev - p) * mu
    o1, o2, o3 = RWKV_WIDTH, 2 * RWKV_WIDTH, 3 * RWKV_WIDTH
    o4 = o3 + RWKV_DECAY_LORA
    o5 = o4 + RWKV_A_LORA
    r, k, v = p[..., :o1], p[..., o1:o2], p[..., o2:o3]
    xw, xa, xg = p[..., o3:o4], p[..., o4:o5], p[..., o5:]
    w = -jax.nn.softplus(-(w0 + jnp.tanh(xw) @ w2).astype(f32)) - 0.5
    decay = jnp.exp(-jnp.exp(w))
    a = jax.nn.sigmoid((a0 + xa @ a2).astype(f32))
    g = (jax.nn.sigmoid(xg) @ g2).astype(f32)
    k = k.astype(f32)
    heads = lambda t: t.astype(f32).reshape(B, S, H, N)
    kk = l2norm(heads(k * k_k))
    k_mod = heads(k * (1.0 + (a - 1.0) * k_a))
    r_h, v_h, a_h = heads(r), heads(v), heads(a)
    xs = tuple(jnp.moveaxis(t, 1, 0) for t in (r_h, heads(decay), k_mod, v_h, kk, kk * a_h))

    def step(state, inp):
        r_t, w_t, k_t, v_t, kk_t, kka_t = inp
        sa = jnp.einsum("bhij,bhj->bhi", state, kk_t)
        state = (state * w_t[:, :, None, :] - sa[..., None] * kka_t[:, :, None, :]
                 + v_t[..., None] * k_t[:, :, None, :])
        return state, jnp.einsum("bhij,bhj->bhi", state, r_t)

    _, o = lax.scan(step, jnp.zeros((B, H, N, N), f32), xs)
    o = jnp.moveaxis(o, 0, 1)
    mean = jnp.mean(o, axis=-1, keepdims=True)
    var = jnp.mean(jnp.square(o - mean), axis=-1, keepdims=True)
    o = ((o - mean) * lax.rsqrt(var + RWKV_GN_EPS)).reshape(B, S, RWKV_WIDTH) * ln_w + ln_b
    bonus = jnp.sum(r_h * k_mod * r_k, axis=-1, keepdims=True) * v_h
    o = o + bonus.reshape(B, S, RWKV_WIDTH)
    return (o * g).astype(p.dtype)


def chunk_gated_delta_rule(q, k, v, beta, g):
    B, S, H, Dk = q.shape
    Dv = v.shape[-1]
    nc = S // CHUNK

    def blocks(t):
        return jnp.moveaxis(t.reshape((B, nc, CHUNK, H) + t.shape[3:]), 3, 1)

    q, k, v, beta, g = map(blocks, (q, k, v, beta, g))
    g_cum = jnp.cumsum(g, axis=-1)
    decay = segsum_exp(g)
    kb = k * beta[..., None]
    strict = jnp.tril(jnp.ones((CHUNK, CHUNK), bool), -1)
    lower = jnp.where(strict, jnp.einsum("bhnid,bhnjd->bhnij", kb, k) * decay, 0.0)
    solve = functools.partial(lax.linalg.triangular_solve, left_side=True, lower=True,
                              unit_diagonal=True)
    u = solve(lower, v * beta[..., None])
    w = solve(lower, kb * jnp.exp(g_cum)[..., None])
    qk = jnp.einsum("bhnid,bhnjd->bhnij", q, k) * decay
    g_last = g_cum[..., -1]
    k_tail = k * jnp.exp(g_last[..., None] - g_cum)[..., None]
    q_head = q * jnp.exp(g_cum)[..., None]
    xs = tuple(jnp.moveaxis(t, 2, 0) for t in (q_head, k_tail, u, w, qk, g_last))

    def step(state, inp):
        qh, kt, uc, wc, qkc, gl = inp
        v_new = uc - jnp.einsum("bhck,bhkv->bhcv", wc, state)
        o = jnp.einsum("bhck,bhkv->bhcv", qh, state) + jnp.einsum("bhij,bhjv->bhiv", qkc, v_new)
        state = state * jnp.exp(gl)[..., None, None] + jnp.einsum("bhck,bhcv->bhkv", kt, v_new)
        return state, o

    _, o = lax.scan(step, jnp.zeros((B, H, Dk, Dv), q.dtype), xs)
    return jnp.transpose(o, (1, 0, 3, 2, 4)).reshape(B, S, H, Dv)


def gated_deltanet(p, conv_w, a_log, dt_bias, norm_w):
    B, S, _ = p.shape
    H, Dh, W = GDN_HEADS, GDN_HEAD, GDN_WIDTH
    f32 = jnp.float32
    qkv = jax.nn.silu(causal_dwconv(p[..., :3 * W], conv_w)).astype(f32)
    b_p = p[..., 3 * W:3 * W + H].astype(f32)
    a_p = p[..., 3 * W + H:3 * W + 2 * H].astype(f32)
    z = p[..., 3 * W + 2 * H:].astype(f32).reshape(B, S, H, Dh)
    q = l2norm(qkv[..., :W].reshape(B, S, H, Dh)) * Dh ** -0.5
    k = l2norm(qkv[..., W:2 * W].reshape(B, S, H, Dh))
    v = qkv[..., 2 * W:].reshape(B, S, H, Dh)
    beta = jax.nn.sigmoid(b_p)
    g = -jnp.exp(a_log.astype(f32)) * jax.nn.softplus(a_p + dt_bias.astype(f32))
    o = chunk_gated_delta_rule(q, k, v, beta, g)
    o = rms_norm(o, norm_w) * jax.nn.silu(z)
    return o.reshape(B, S, W).astype(p.dtype)


def mamba2_ssd(p, conv_w, conv_b, dt_bias, a_log, d_skip, norm_w):
    B, S, _ = p.shape
    W, G, N, P, C = SSD_WIDTH, SSD_GROUPS, SSD_STATE, SSD_HEAD, CHUNK
    HG = SSD_HEADS // G
    nc = S // C
    f32 = jnp.float32
    z = p[..., :W].astype(f32)
    xbc = jax.nn.silu(causal_dwconv(p[..., W:W + SSD_CONV_DIM], conv_w) + conv_b).astype(f32)
    dt_p = p[..., W + SSD_CONV_DIM:].astype(f32)
    x = xbc[..., :W].reshape(B, nc, C, G, HG, P)
    Bm = xbc[..., W:W + G * N].reshape(B, nc, C, G, N)
    Cm = xbc[..., W + G * N:].reshape(B, nc, C, G, N)
    dt = jax.nn.softplus(dt_p + dt_bias.astype(f32)).reshape(B, nc, C, G, HG)
    A = -jnp.exp(a_log.astype(f32)).reshape(G, HG)
    a = jnp.transpose(dt * A, (0, 3, 4, 1, 2))
    a_cum = jnp.cumsum(a, axis=-1)
    xdt = x * dt[..., None]
    cb = jnp.einsum("bclgn,bcsgn->bgcls", Cm, Bm)
    y_diag = jnp.einsum("bgcls,bghcls,bcsghp->bclghp", cb, segsum_exp(a), xdt)
    decay_states = jnp.exp(a_cum[..., -1:] - a_cum)
    states = jnp.einsum("bclgn,bghcl,bclghp->bcghpn", Bm, decay_states, xdt)
    states = jnp.concatenate([jnp.zeros_like(states[:, :1]), states], axis=1)
    chunk_decay = jnp.pad(a_cum[..., -1], ((0, 0), (0, 0), (0, 0), (1, 0)))
    new_states = jnp.einsum("bghzc,bcghpn->bzghpn", segsum_exp(chunk_decay), states)
    y_off = jnp.einsum("bclgn,bcghpn,bghcl->bclghp", Cm, new_states[:, :-1], jnp.exp(a_cum))
    y = y_diag + y_off + d_skip.astype(f32).reshape(G, HG)[..., None] * x
    y = y.reshape(B, S, W) * jax.nn.silu(z)
    y = rms_norm(y.reshape(B, S, G, W // G), norm_w.reshape(G, W // G))
    return y.reshape(B, S, W).astype(p.dtype)


def swiglu(h, w_gu, w_down):
    g, u = jnp.split(h @ w_gu, 2, axis=-1)
    return (jax.nn.silu(g) * u) @ w_down


def moe_swiglu(h, router, w_gu, w_down):
    B, S, D = h.shape
    T = B * S
    A = T * TOP_K
    n_blk = -(-(A + N_EXPERTS * (MOE_BLOCK - 1)) // MOE_BLOCK)
    n_slot = n_blk * MOE_BLOCK
    hf = h.reshape(T, D)
    logits = (hf @ router).astype(jnp.float32)
    top_logit, top_idx = lax.top_k(logits, TOP_K)
    top_w = jax.nn.softmax(top_logit, axis=-1)
    flat_e = top_idx.reshape(A)
    flat_tok = jnp.repeat(jnp.arange(T, dtype=jnp.int32), TOP_K)
    flat_w = top_w.reshape(A)
    order = jnp.argsort(flat_e)
    e_sorted = flat_e[order]
    counts = jnp.bincount(flat_e, length=N_EXPERTS)
    padded = (counts + MOE_BLOCK - 1) // MOE_BLOCK * MOE_BLOCK
    start = jnp.cumsum(counts) - counts
    pad_end = jnp.cumsum(padded)
    pad_start = pad_end - padded
    dest = pad_start[e_sorted] + jnp.arange(A, dtype=jnp.int32) - start[e_sorted]
    slot_tok = jnp.zeros((n_slot,), jnp.int32).at[dest].set(flat_tok[order])
    slot_w = jnp.zeros((n_slot,), jnp.float32).at[dest].set(flat_w[order])
    blk_e = jnp.minimum(jnp.searchsorted(pad_end, jnp.arange(n_blk, dtype=jnp.int32) * MOE_BLOCK,
                                         side="right"), N_EXPERTS - 1)
    xb = hf[slot_tok].reshape(n_blk, MOE_BLOCK, D)

    def expert_block(args):
        xi, e = args
        g, u = jnp.split(xi @ w_gu[e], 2, axis=-1)
        return (jax.nn.silu(g) * u) @ w_down[e]

    yb = lax.map(expert_block, (xb, blk_e)).reshape(n_slot, D)
    out = jnp.zeros((T, D), jnp.float32).at[slot_tok].add(yb.astype(jnp.float32) * slot_w[:, None])
    return out.astype(h.dtype).reshape(B, S, D)


def setup_inputs(seed: int = 0) -> dict:
    key = jax.random.key(seed)
    keys = iter(jax.random.split(key, 48))
    f32 = jnp.float32

    def normal(shape, scale):
        return scale * jax.random.normal(next(keys), shape, f32)

    def gain(shape):
        return 1.0 + 0.02 * jax.random.normal(next(keys), shape, f32)

    def uniform(shape, lo, hi):
        return jax.random.uniform(next(keys), shape, f32, lo, hi)

    def dt_bias(shape):
        dt = jnp.exp(uniform(shape, math.log(DT_MIN), math.log(DT_MAX)))
        return dt + jnp.log(-jnp.expm1(-dt))

    L, nd, nm = DEPTH, (DEPTH + 1) // 2, DEPTH // 2
    start = jax.random.randint(next(keys), (BATCH, 1), 0, MAX_START, dtype=jnp.int32)
    return {
        "x": normal((BATCH, SEQ, D_MODEL), 1.0),
        "positions": start + jnp.arange(SEQ, dtype=jnp.int32)[None, :],
        "norm_mix": gain((L, D_MODEL)),
        "w_in": normal((L, D_MODEL, IN_COLS), D_MODEL ** -0.5),
        "mla_q_norm": gain((L, MLA_Q_LORA)),
        "mla_kv_norm": gain((L, MLA_KV_LORA)),
        "mla_w_uq": normal((L, MLA_Q_LORA, MLA_HEADS * (MLA_NOPE + MLA_ROPE)), MLA_Q_LORA ** -0.5),
        "mla_w_ukv": normal((L, MLA_KV_LORA, MLA_HEADS * (MLA_NOPE + MLA_V)), MLA_KV_LORA ** -0.5),
        "rwkv_mu": uniform((L, RWKV_COLS), 0.0, 1.0),
        "rwkv_w0": uniform((L, RWKV_WIDTH), -3.0, 0.0),
        "rwkv_w2": normal((L, RWKV_DECAY_LORA, RWKV_WIDTH), 0.1),
        "rwkv_a0": normal((L, RWKV_WIDTH), 0.1),
        "rwkv_a2": normal((L, RWKV_A_LORA, RWKV_WIDTH), 0.1),
        "rwkv_g2": normal((L, RWKV_GATE_LORA, RWKV_WIDTH), RWKV_GATE_LORA ** -0.5),
        "rwkv_k_k": 0.85 + normal((L, RWKV_WIDTH), 0.05),
        "rwkv_k_a": 1.0 + normal((L, RWKV_WIDTH), 0.05),
        "rwkv_r_k": normal((L, RWKV_HEADS, RWKV_HEAD), 0.1),
        "rwkv_ln_w": gain((L, RWKV_WIDTH)),
        "rwkv_ln_b": normal((L, RWKV_WIDTH), 0.02),
        "gdn_conv": normal((L, CONV_K, 3 * GDN_WIDTH), CONV_K ** -0.5),
        "gdn_a_log": jnp.log(uniform((L, GDN_HEADS), 1.0, 16.0)),
        "gdn_dt_bias": dt_bias((L, GDN_HEADS)),
        "gdn_norm": gain((L, GDN_HEAD)),
        "ssd_conv_w": normal((L, CONV_K, SSD_CONV_DIM), CONV_K ** -0.5),
        "ssd_conv_b": normal((L, SSD_CONV_DIM), 0.02),
        "ssd_dt_bias": dt_bias((L, SSD_HEADS)),
        "ssd_a_log": jnp.log(uniform((L, SSD_HEADS), 1.0, 16.0)),
        "ssd_d": gain((L, SSD_HEADS)),
        "ssd_norm": gain((L, SSD_WIDTH)),
        "gate_b": normal((L, N_BRANCH, D_MODEL), 0.02),
        "w_branch": normal((L, N_BRANCH, BRANCH_WIDTH, D_MODEL), BRANCH_WIDTH ** -0.5),
        "w_out": normal((L, D_MODEL, D_MODEL), D_MODEL ** -0.5),
        "norm_ffn": gain((L, D_MODEL)),
        "ffn_w_gu": normal((nd, D_MODEL, 2 * FF_DENSE), D_MODEL ** -0.5),
        "ffn_w_down": normal((nd, FF_DENSE, D_MODEL), FF_DENSE ** -0.5),
        "moe_router": normal((nm, D_MODEL, N_EXPERTS), D_MODEL ** -0.5),
        "moe_w_gu": normal((nm, N_EXPERTS, D_MODEL, 2 * FF_EXPERT), D_MODEL ** -0.5),
        "moe_w_down": normal((nm, N_EXPERTS, FF_EXPERT, D_MODEL), FF_EXPERT ** -0.5),
        "norm_final": gain((D_MODEL,)),
    }


def reference(x, positions, norm_mix, w_in, mla_q_norm, mla_kv_norm, mla_w_uq, mla_w_ukv,
              rwkv_mu, rwkv_w0, rwkv_w2, rwkv_a0, rwkv_a2, rwkv_g2, rwkv_k_k, rwkv_k_a, rwkv_r_k,
              rwkv_ln_w, rwkv_ln_b, gdn_conv, gdn_a_log, gdn_dt_bias, gdn_norm,
              ssd_conv_w, ssd_conv_b, ssd_dt_bias, ssd_a_log, ssd_d, ssd_norm,
              gate_b, w_branch, w_out, norm_ffn, ffn_w_gu, ffn_w_down,
              moe_router, moe_w_gu, moe_w_down, norm_final):
    for layer in range(DEPTH):
        h = rms_norm(x, norm_mix[layer])
        proj = h @ w_in[layer]
        y_a = mla_attention(proj[..., :OFF_RWKV], positions, mla_q_norm[layer], mla_kv_norm[layer],
                            mla_w_uq[layer], mla_w_ukv[layer])
        y_b = rwkv7_time_mix(proj[..., OFF_RWKV:OFF_GDN], rwkv_mu[layer], rwkv_w0[layer],
                             rwkv_w2[layer], rwkv_a0[layer], rwkv_a2[layer], rwkv_g2[layer],
                             rwkv_k_k[layer], rwkv_k_a[layer], rwkv_r_k[layer],
                             rwkv_ln_w[layer], rwkv_ln_b[layer])
        y_c = gated_deltanet(proj[..., OFF_GDN:OFF_SSD], gdn_conv[layer], gdn_a_log[layer],
                             gdn_dt_bias[layer], gdn_norm[layer])
        y_d = mamba2_ssd(proj[..., OFF_SSD:OFF_GATE], ssd_conv_w[layer], ssd_conv_b[layer],
                         ssd_dt_bias[layer], ssd_a_log[layer], ssd_d[layer], ssd_norm[layer])
        merged = None
        for i, y_i in enumerate((y_a, y_b, y_c, y_d)):
            g0 = OFF_GATE + i * D_MODEL
            gate = jax.nn.sigmoid(proj[..., g0:g0 + D_MODEL] + gate_b[layer, i])
            term = gate * (y_i @ w_branch[layer, i])
            merged = term if merged is None else merged + term
        x = x + merged @ w_out[layer]
        h = rms_norm(x, norm_ffn[layer])
        if layer % 2 == 0:
            f = swiglu(h, ffn_w_gu[layer // 2], ffn_w_down[layer // 2])
        else:
            f = moe_swiglu(h, moe_router[layer // 2], moe_w_gu[layer // 2], moe_w_down[layer // 2])
        x = x + f
    return rms_norm(x, norm_final)
```

```python
import functools
import math

import jax
import jax.numpy as jnp
from jax import lax
from jax.experimental import pallas as pl
from jax.experimental.pallas import tpu as pltpu

F32 = jnp.float32
BF16 = jnp.bfloat16

D_MODEL = 1024
CHUNK = 64
EPS = 1e-6
MLA_HEADS = 4
MLA_Q_LORA = 384
MLA_KV_LORA = 256
MLA_NOPE = 128
MLA_ROPE = 64
MLA_V = 128
ROPE_THETA = 10000.0
RWKV_HEADS = 8
RWKV_HEAD = 64
RWKV_WIDTH = RWKV_HEADS * RWKV_HEAD
RWKV_DECAY_LORA = 64
RWKV_A_LORA = 64
RWKV_GATE_LORA = 160
RWKV_GN_EPS = 64e-5
GDN_HEADS = 4
GDN_HEAD = 128
GDN_WIDTH = GDN_HEADS * GDN_HEAD
CONV_K = 4
SSD_HEADS = 8
SSD_HEAD = 64
SSD_WIDTH = SSD_HEADS * SSD_HEAD
SSD_GROUPS = 2
SSD_STATE = 128
SSD_CONV_DIM = SSD_WIDTH + 2 * SSD_GROUPS * SSD_STATE
N_BRANCH = 4
BRANCH_WIDTH = 512
FF_DENSE = 2816
N_EXPERTS = 8
TOP_K = 2
FF_EXPERT = 3584
MOE_BLOCK = 512

MLA_COLS = MLA_Q_LORA + MLA_KV_LORA + MLA_ROPE
RWKV_COLS = 3 * RWKV_WIDTH + RWKV_DECAY_LORA + RWKV_A_LORA + RWKV_GATE_LORA
GDN_COLS = 4 * GDN_WIDTH + 2 * GDN_HEADS
SSD_COLS = SSD_WIDTH + SSD_CONV_DIM + SSD_HEADS
OFF_RWKV = MLA_COLS
OFF_GDN = OFF_RWKV + RWKV_COLS
OFF_SSD = OFF_GDN + GDN_COLS
OFF_GATE = OFF_SSD + SSD_COLS

LANES = 128
SUBLANES = 8
VMEM_LIMIT_BYTES = 56 * 2**20

ROW_TILE = 256
SEQ_TILE = 256
ATT_TILE = 256
PAD_ROWS = SUBLANES

MLA_P = MLA_Q_LORA + MLA_KV_LORA + 2 * LANES
RWKV_P = 3 * RWKV_WIDTH + LANES + 2 * LANES
GDN_P = 4 * GDN_WIDTH + LANES
SSD_P = SSD_WIDTH + SSD_CONV_DIM + LANES


def _cparams(*sem):
    return pltpu.CompilerParams(dimension_semantics=sem, vmem_limit_bytes=VMEM_LIMIT_BYTES)


def _resident(shape):
    nd = len(shape)
    return pl.BlockSpec(shape, lambda *_: (0,) * nd, pipeline_mode=pl.Buffered(1))


def _rows(tile, width):
    return pl.BlockSpec((tile, width), lambda i: (i, 0))


def _bdot(a, b):
    return jnp.dot(a.astype(BF16), b.astype(BF16), preferred_element_type=F32)


def _bdot_nt(a, b):
    return lax.dot_general(a.astype(BF16), b.astype(BF16), (((1,), (1,)), ((), ())),
                           preferred_element_type=F32)


def _bdot_tn(a, b):
    return lax.dot_general(a.astype(BF16), b.astype(BF16), (((0,), (0,)), ((), ())),
                           preferred_element_type=F32)


def _split_dot(m01, x):
    hi = x.astype(BF16)
    lo = (x - hi.astype(F32)).astype(BF16)
    return (jnp.dot(m01, hi, preferred_element_type=F32)
            + jnp.dot(m01, lo, preferred_element_type=F32))


def _dot_split(x, m01):
    hi = x.astype(BF16)
    lo = (x - hi.astype(F32)).astype(BF16)
    return (jnp.dot(hi, m01, preferred_element_type=F32)
            + jnp.dot(lo, m01, preferred_element_type=F32))


def _rms(x, g):
    return x * lax.rsqrt(jnp.mean(x * x, axis=-1, keepdims=True) + EPS) * g


def _sigmoid(x):
    return 1.0 / (1.0 + jnp.exp(-x))


def _silu(x):
    return x * _sigmoid(x)


def _softplus(x):
    return jnp.maximum(x, 0.0) + jnp.log(1.0 + jnp.exp(-jnp.abs(x)))


def _iota2(shape, dim):
    return lax.broadcasted_iota(jnp.int32, shape, dim)


def _tri_inv(low, steps):
    n = low.shape[0]
    eye = (_iota2((n, n), 0) == _iota2((n, n), 1)).astype(F32)
    p = eye - low
    q = _bdot(low, low)
    for _ in range(steps - 2):
        r = _bdot(jnp.concatenate([p, q], axis=0), q)
        p = p + r[:n]
        q = r[n:]
    return p + _bdot(p, q)


def _norm_proj_kernel(x_ref, g_ref, wa_ref, wb_ref, wc_ref, wd_ref, oa_ref, ob_ref, oc_ref, od_ref):
    h = _rms(x_ref[...], g_ref[...]).astype(BF16)
    for w_ref, o_ref in ((wa_ref, oa_ref), (wb_ref, ob_ref), (wc_ref, oc_ref), (wd_ref, od_ref)):
        o_ref[...] = jnp.dot(h, w_ref[...], preferred_element_type=F32)


def norm_proj(x, g, weights):
    t = x.shape[0]
    widths = [w.shape[1] for w in weights]
    return pl.pallas_call(
        _norm_proj_kernel,
        grid=(t // ROW_TILE,),
        in_specs=[_rows(ROW_TILE, D_MODEL), _resident((1, D_MODEL))] + [_resident(w.shape) for w in weights],
        out_specs=[_rows(ROW_TILE, n) for n in widths],
        out_shape=[jax.ShapeDtypeStruct((t, n), F32) for n in widths],
        compiler_params=_cparams("parallel"),
        name="norm_proj",
    )(x, g, *weights)


def _rope_table_kernel(pos_ref, freq_ref, cos_ref, sin_ref):
    ang = pos_ref[...].astype(F32) * freq_ref[...]
    cos_ref[...] = jnp.cos(ang)
    sin_ref[...] = jnp.sin(ang)


def rope_tables(pos_col):
    t = pos_col.shape[0]
    half = MLA_ROPE // 2
    inv_freq = ROPE_THETA ** (-jnp.arange(half, dtype=F32) / half)
    freq = jnp.concatenate([inv_freq, inv_freq, jnp.zeros((LANES - MLA_ROPE,), F32)])[None, :]
    tile = min(1024, t)
    return pl.pallas_call(
        _rope_table_kernel,
        grid=(t // tile,),
        in_specs=[_rows(tile, 1), _resident((1, LANES))],
        out_specs=[_rows(tile, LANES), _rows(tile, LANES)],
        out_shape=[jax.ShapeDtypeStruct((t, LANES), F32)] * 2,
        compiler_params=_cparams("parallel"),
        name="rope_tables",
    )(pos_col, freq)


def _mla_prep_kernel(p_ref, cos_ref, sin_ref, qn_ref, kvn_ref, wq_ref, wqr_ref, wk_ref, wv_ref,
                     q_ref, k_ref, v_ref):
    cos = cos_ref[...]
    sin = sin_ref[...]
    scale = (MLA_NOPE + MLA_ROPE) ** -0.5
    nq = _rms(p_ref[:, :MLA_Q_LORA], qn_ref[...]).astype(BF16)
    q = jnp.dot(nq, wq_ref[...], preferred_element_type=F32)
    qr = jnp.dot(nq, wqr_ref[...], preferred_element_type=F32)
    nkv = _rms(p_ref[:, MLA_Q_LORA:MLA_Q_LORA + MLA_KV_LORA], kvn_ref[...]).astype(BF16)
    kn = jnp.dot(nkv, wk_ref[...], preferred_element_type=F32)
    v_ref[...] = jnp.dot(nkv, wv_ref[...], preferred_element_type=F32).astype(BF16)
    o0 = MLA_Q_LORA + MLA_KV_LORA
    k_rope = (p_ref[:, o0:o0 + LANES] * cos + p_ref[:, o0 + LANES:o0 + 2 * LANES] * sin).astype(BF16)
    for h in range(MLA_HEADS):
        c0 = 2 * LANES * h
        q_ref[:, c0:c0 + LANES] = (q[:, c0:c0 + LANES] * scale).astype(BF16)
        q_rope = q[:, c0 + LANES:c0 + 2 * LANES] * cos + qr[:, h * LANES:(h + 1) * LANES] * sin
        q_ref[:, c0 + LANES:c0 + 2 * LANES] = (q_rope * scale).astype(BF16)
        k_ref[:, c0:c0 + LANES] = kn[:, h * LANES:(h + 1) * LANES].astype(BF16)
        k_ref[:, c0 + LANES:c0 + 2 * LANES] = k_rope


def _mla_attn_kernel(q_ref, k_ref, v_ref, o_ref):
    i = pl.program_id(2)
    q = q_ref[...]
    tq = q.shape[0]

    def scores(j):
        k = k_ref[pl.ds(pl.multiple_of(j * ATT_TILE, ATT_TILE), ATT_TILE), :]
        return lax.dot_general(q, k, (((1,), (1,)), ((), ())), preferred_element_type=F32)

    def update(carry, s, j):
        m, l, acc = carry
        m_new = jnp.maximum(m, jnp.max(s, axis=-1, keepdims=True))
        alpha = jnp.exp(m - m_new)
        p = jnp.exp(s - m_new)
        v = v_ref[pl.ds(pl.multiple_of(j * ATT_TILE, ATT_TILE), ATT_TILE), :]
        acc = acc * alpha + jnp.dot(p.astype(BF16), v, preferred_element_type=F32)
        return m_new, l * alpha + jnp.sum(p, axis=-1, keepdims=True), acc

    def body(j, carry):
        return update(carry, scores(j), j)

    init = (jnp.full((tq, 1), -jnp.inf, F32), jnp.zeros((tq, 1), F32), jnp.zeros((tq, MLA_V), F32))
    carry = lax.fori_loop(0, i, body, init)
    allowed = (_iota2((tq, ATT_TILE), 1) // CHUNK) <= (_iota2((tq, ATT_TILE), 0) // CHUNK)
    s = jnp.where(allowed, scores(i), -jnp.inf)
    _, l, acc = update(carry, s, i)
    o_ref[...] = (acc / l).astype(o_ref.dtype)


def mla_attention(p, cos, sin, q_norm, kv_norm, wq, wqr, wk, wv, batch, seq):
    t = p.shape[0]
    hq = 2 * LANES
    q, k, v = pl.pallas_call(
        _mla_prep_kernel,
        grid=(t // ROW_TILE,),
        in_specs=[_rows(ROW_TILE, MLA_P), _rows(ROW_TILE, LANES), _rows(ROW_TILE, LANES),
                  _resident(q_norm.shape), _resident(kv_norm.shape), _resident(wq.shape),
                  _resident(wqr.shape), _resident(wk.shape), _resident(wv.shape)],
        out_specs=[_rows(ROW_TILE, MLA_HEADS * hq), _rows(ROW_TILE, MLA_HEADS * hq),
                   _rows(ROW_TILE, MLA_HEADS * MLA_V)],
        out_shape=[jax.ShapeDtypeStruct((t, MLA_HEADS * hq), BF16),
                   jax.ShapeDtypeStruct((t, MLA_HEADS * hq), BF16),
                   jax.ShapeDtypeStruct((t, MLA_HEADS * MLA_V), BF16)],
        compiler_params=_cparams("parallel"),
        name="mla_prep",
    )(p, cos, sin, q_norm, kv_norm, wq, wqr, wk, wv)
    nq = seq // ATT_TILE
    return pl.pallas_call(
        _mla_attn_kernel,
        grid=(batch, MLA_HEADS, nq),
        in_specs=[pl.BlockSpec((ATT_TILE, hq), lambda b, h, i: (b * nq + i, h)),
                  pl.BlockSpec((seq, hq), lambda b, h, i: (b, h)),
                  pl.BlockSpec((seq, MLA_V), lambda b, h, i: (b, h))],
        out_specs=pl.BlockSpec((ATT_TILE, MLA_V), lambda b, h, i: (b * nq + i, h)),
        out_shape=jax.ShapeDtypeStruct((t, MLA_HEADS * MLA_V), BF16),
        compiler_params=_cparams("parallel", "parallel", "arbitrary"),
        name="mla_attn",
    )(q, k, v)


def _mla_weights(w_in_l, w_uq, w_ukv):
    half = MLA_ROPE // 2
    zpad = lambda rows, n: jnp.zeros((rows, n), F32)
    o0 = MLA_Q_LORA + MLA_KV_LORA
    kr = w_in_l[:, o0:o0 + MLA_ROPE]
    kr_rot = jnp.concatenate([-kr[:, half:], kr[:, :half]], axis=1)
    w_a = jnp.concatenate([w_in_l[:, :o0], kr, zpad(D_MODEL, LANES - MLA_ROPE),
                           kr_rot, zpad(D_MODEL, LANES - MLA_ROPE)], axis=1)
    uq = w_uq.reshape(MLA_Q_LORA, MLA_HEADS, MLA_NOPE + MLA_ROPE)
    rope_w = uq[:, :, MLA_NOPE:]
    wq = jnp.concatenate([uq, jnp.zeros((MLA_Q_LORA, MLA_HEADS, LANES - MLA_ROPE), F32)], axis=2)
    wqr = jnp.concatenate([-rope_w[:, :, half:], rope_w[:, :, :half],
                           jnp.zeros((MLA_Q_LORA, MLA_HEADS, LANES - MLA_ROPE), F32)], axis=2)
    ukv = w_ukv.reshape(MLA_KV_LORA, MLA_HEADS, MLA_NOPE + MLA_V)
    wk = ukv[:, :, :MLA_NOPE].reshape(MLA_KV_LORA, MLA_HEADS * MLA_NOPE)
    wv = ukv[:, :, MLA_NOPE:].reshape(MLA_KV_LORA, MLA_HEADS * MLA_V)
    return (w_a.astype(BF16), wq.reshape(MLA_Q_LORA, -1).astype(BF16),
            wqr.reshape(MLA_Q_LORA, -1).astype(BF16), wk.astype(BF16), wv.astype(BF16))


def _merge_kernel(x_ref, g_ref, wg_ref, gb_ref, ya_ref, yb_ref, yc_ref, yd_ref, wb_ref, wo_ref, o_ref):
    x = x_ref[...]
    h = _rms(x, g_ref[...]).astype(BF16)
    merged = None
    for i, y_ref in enumerate((ya_ref, yb_ref, yc_ref, yd_ref)):
        gate = _sigmoid(jnp.dot(h, wg_ref[i], preferred_element_type=F32) + gb_ref[i:i + 1, :])
        term = gate * jnp.dot(y_ref[...], wb_ref[i], preferred_element_type=F32)
        merged = term if merged is None else merged + term
    o_ref[...] = x + jnp.dot(merged.astype(BF16), wo_ref[...], preferred_element_type=F32)


def merge(x, g, w_gate, gate_b, ys, w_branch, w_out):
    t = x.shape[0]
    return pl.pallas_call(
        _merge_kernel,
        grid=(t // ROW_TILE,),
        in_specs=[_rows(ROW_TILE, D_MODEL), _resident((1, D_MODEL)), _resident(w_gate.shape),
                  _resident(gate_b.shape)] + [_rows(ROW_TILE, BRANCH_WIDTH)] * N_BRANCH
                 + [_resident(w_branch.shape), _resident(w_out.shape)],
        out_specs=_rows(ROW_TILE, D_MODEL),
        out_shape=jax.ShapeDtypeStruct((t, D_MODEL), F32),
        compiler_params=_cparams("parallel"),
        name="merge",
    )(x, g, w_gate, gate_b, *ys, w_branch, w_out)


FF_TILE = FF_DENSE // 2


def _ffn_kernel(x_ref, g_ref, wgu_ref, wd_ref, o_ref):
    x = x_ref[...]
    h = _rms(x, g_ref[...]).astype(BF16)
    acc = x
    for j in range(FF_DENSE // FF_TILE):
        c0 = j * FF_TILE
        gate = jnp.dot(h, wgu_ref[:, c0:c0 + FF_TILE], preferred_element_type=F32)
        up = jnp.dot(h, wgu_ref[:, FF_DENSE + c0:FF_DENSE + c0 + FF_TILE], preferred_element_type=F32)
        act = (_silu(gate) * up).astype(BF16)
        acc = acc + jnp.dot(act, wd_ref[c0:c0 + FF_TILE, :], preferred_element_type=F32)
    o_ref[...] = acc


def ffn_dense(x, g, w_gu, w_down):
    t = x.shape[0]
    return pl.pallas_call(
        _ffn_kernel,
        grid=(t // ROW_TILE,),
        in_specs=[_rows(ROW_TILE, D_MODEL), _resident((1, D_MODEL)), _resident(w_gu.shape),
                  _resident(w_down.shape)],
        out_specs=_rows(ROW_TILE, D_MODEL),
        out_shape=jax.ShapeDtypeStruct((t, D_MODEL), F32),
        compiler_params=_cparams("parallel"),
        name="ffn_dense",
    )(x, g, w_gu, w_down)


def _final_norm_kernel(x_ref, g_ref, o_ref):
    o_ref[...] = _rms(x_ref[...], g_ref[...])


def final_norm(x, g):
    t = x.shape[0]
    tile = min(1024, t)
    return pl.pallas_call(
        _final_norm_kernel,
        grid=(t // tile,),
        in_specs=[_rows(tile, D_MODEL), _resident((1, D_MODEL))],
        out_specs=_rows(tile, D_MODEL),
        out_shape=jax.ShapeDtypeStruct((t, D_MODEL), F32),
        compiler_params=_cparams("parallel"),
        name="final_norm",
    )(x, g)


def _causal_conv(buf_ref, x, w_ref):
    tile = x.shape[0]
    buf_ref[PAD_ROWS:PAD_ROWS + tile, :] = x
    acc = None
    for j in range(CONV_K):
        off = PAD_ROWS - (CONV_K - 1) + j
        term = buf_ref[off:off + tile, :] * w_ref[j:j + 1, :]
        acc = term if acc is None else acc + term
    buf_ref[0:PAD_ROWS, :] = buf_ref[tile:tile + PAD_ROWS, :]
    return acc


def _chunk_tri(tile):
    r = _iota2((tile, tile), 0)
    c = _iota2((tile, tile), 1)
    return jnp.where(c <= r, jnp.where(r // CHUNK == c // CHUNK, 1.0, 0.0), 0.0).astype(BF16)


def _head_expand(width):
    n = LANES * width
    return jnp.where(_iota2((LANES, n), 1) // width == _iota2((LANES, n), 0), 1.0, 0.0).astype(BF16)


def _ssd_kernel(p_ref, cw_ref, cb_ref, dtb_ref, a_ref, d_ref, nw_ref, o_ref, buf_ref, y_ref, h_ref):
    ts = SEQ_TILE
    c_len = CHUNK
    w = SSD_WIDTH
    gn = SSD_GROUPS * SSD_STATE

    @pl.when(pl.program_id(1) == 0)
    def _():
        buf_ref[0:PAD_ROWS, :] = jnp.zeros((PAD_ROWS, SSD_CONV_DIM), F32)
        h_ref[...] = jnp.zeros(h_ref.shape, F32)

    z = p_ref[:, :w]
    xbc = _silu(_causal_conv(buf_ref, p_ref[:, w:w + SSD_CONV_DIM], cw_ref) + cb_ref[...])
    x = xbc[:, :w]
    bm = xbc[:, w:w + gn]
    cm = xbc[:, w + gn:]
    dt = _softplus(p_ref[:, w + SSD_CONV_DIM:] + dtb_ref[...])
    a = dt * a_ref[...]
    acum = _split_dot(_chunk_tri(ts), a)
    expand = _head_expand(SSD_HEAD)[:, :w]
    dt_x = _dot_split(dt, expand)
    acum_x = _dot_split(acum, expand)
    xdt = x * dt_x
    acum_next = pltpu.roll(acum, LANES - 1, axis=1)
    lane = _iota2((c_len, LANES), 1)
    left = lane < SSD_HEAD
    causal2 = (lane % c_len) <= _iota2((c_len, LANES), 0)

    for c in range(ts // c_len):
        r0 = c * c_len
        ac = acum[r0:r0 + c_len]
        at = jnp.concatenate([ac, acum_next[r0:r0 + c_len]], axis=0).T
        ax = acum_x[r0:r0 + c_len]
        a_last = ax[c_len - 1:c_len, :]
        e_in = jnp.exp(ax)
        xdt_c = xdt[r0:r0 + c_len]
        xdt_d = xdt_c * jnp.exp(a_last - ax)
        e_last = jnp.exp(a_last)
        for g in range(SSD_GROUPS):
            bm_g = bm[r0:r0 + c_len, g * SSD_STATE:(g + 1) * SSD_STATE]
            cm_g = cm[r0:r0 + c_len, g * SSD_STATE:(g + 1) * SSD_STATE]
            cb2 = _bdot_nt(cm_g, jnp.concatenate([bm_g, bm_g], axis=0))
            for pp in range(SSD_HEADS // SSD_GROUPS // 2):
                p = g * (SSD_HEADS // SSD_GROUPS // 2) + pp
                l0 = p * LANES
                col2 = jnp.where(left, ac[:, 2 * p:2 * p + 1], ac[:, 2 * p + 1:2 * p + 2])
                row2 = at[2 * p:2 * p + 1, :]
                seg = jnp.exp(jnp.where(causal2, col2 - row2, -jnp.inf))
                xp = xdt_c[:, l0:l0 + LANES]
                xs = jnp.concatenate([jnp.where(left, xp, 0.0), jnp.where(left, 0.0, xp)], axis=0)
                y_diag = _bdot(cb2 * seg, xs)
                hp = h_ref[p]
                y_off = _bdot(cm_g, hp) * e_in[:, l0:l0 + LANES]
                h_ref[p] = hp * e_last[:, l0:l0 + LANES] + _bdot_tn(bm_g, xdt_d[:, l0:l0 + LANES])
                y_ref[r0:r0 + c_len, l0:l0 + LANES] = (
                    y_diag + y_off + d_ref[:, l0:l0 + LANES] * x[r0:r0 + c_len, l0:l0 + LANES])

    y = y_ref[...] * _silu(z)
    gw = w // SSD_GROUPS
    for g in range(SSD_GROUPS):
        yg = y[:, g * gw:(g + 1) * gw]
        o_ref[:, g * gw:(g + 1) * gw] = _rms(yg, nw_ref[:, g * gw:(g + 1) * gw]).astype(o_ref.dtype)


def mamba2_ssd(p, conv_w, conv_b, dt_bias, a_log, d_skip, norm_w, batch, seq):
    t = p.shape[0]
    nt = seq // SEQ_TILE
    pad = lambda v: jnp.concatenate([v.astype(F32), jnp.zeros((LANES - v.shape[0],), F32)])[None, :]
    a_row = pad(-jnp.exp(a_log.astype(F32)))
    d_row = jnp.repeat(d_skip.astype(F32), SSD_HEAD)[None, :]
    params = (conv_w, conv_b[None, :], pad(dt_bias), a_row, d_row, norm_w[None, :])
    return pl.pallas_call(
        _ssd_kernel,
        grid=(batch, nt),
        in_specs=[pl.BlockSpec((SEQ_TILE, SSD_P), lambda b, i: (b * nt + i, 0))]
                 + [_resident(v.shape) for v in params],
        out_specs=pl.BlockSpec((SEQ_TILE, SSD_WIDTH), lambda b, i: (b * nt + i, 0)),
        out_shape=jax.ShapeDtypeStruct((t, SSD_WIDTH), BF16),
        scratch_shapes=[pltpu.VMEM((PAD_ROWS + SEQ_TILE, SSD_CONV_DIM), F32),
                        pltpu.VMEM((SEQ_TILE, SSD_WIDTH), F32),
                        pltpu.VMEM((SSD_HEADS // 2, SSD_STATE, LANES), F32)],
        compiler_params=_cparams("parallel", "arbitrary"),
        name="mamba2_ssd",
    )(p, *params)


def _ssd_weights(w_in_l):
    w = w_in_l[:, OFF_SSD:OFF_GATE]
    return jnp.concatenate([w, jnp.zeros((D_MODEL, LANES - SSD_HEADS), F32)], axis=1).astype(BF16)


def _l2n(x):
    return x * lax.rsqrt(jnp.sum(x * x, axis=-1, keepdims=True) + EPS)


def _gdn_kernel(p_ref, cw_ref, bias_ref, arow_ref, nw_ref, o_ref, buf_ref, s_ref):
    ts = SEQ_TILE
    c_len = CHUNK
    w = GDN_WIDTH
    dh = GDN_HEAD

    @pl.when(pl.program_id(1) == 0)
    def _():
        buf_ref[0:PAD_ROWS, :] = jnp.zeros((PAD_ROWS, 3 * w), F32)
        s_ref[...] = jnp.zeros(s_ref.shape, F32)

    qkv = _silu(_causal_conv(buf_ref, p_ref[:, :3 * w], cw_ref))
    ba = p_ref[:, 4 * w:]
    beta = _sigmoid(ba)
    g = arow_ref[...] * _softplus(ba + bias_ref[...])
    gcum = _split_dot(_chunk_tri(ts), g)
    rr = _iota2((c_len, c_len), 0)
    cc = _iota2((c_len, c_len), 1)
    incl = cc <= rr
    strict = cc < rr

    for c in range(ts // c_len):
        r0 = c * c_len
        gc_all = gcum[r0:r0 + c_len]
        gt = jnp.concatenate([gc_all, gc_all], axis=0).T
        for h in range(GDN_HEADS):
            l0 = h * dh
            q = _l2n(qkv[r0:r0 + c_len, l0:l0 + dh]) * dh ** -0.5
            k = _l2n(qkv[r0:r0 + c_len, w + l0:w + l0 + dh])
            v = qkv[r0:r0 + c_len, 2 * w + l0:2 * w + l0 + dh]
            b = beta[r0:r0 + c_len, h:h + 1]
            gcol = gc_all[:, GDN_HEADS + h:GDN_HEADS + h + 1]
            grow = gt[GDN_HEADS + h:GDN_HEADS + h + 1, :c_len]
            decay = jnp.exp(jnp.where(incl, gcol - grow, -jnp.inf))
            kb = k * b
            r = _bdot_nt(jnp.concatenate([kb, q], axis=0), k)
            low = jnp.where(strict, r[:c_len] * decay, 0.0)
            qk = r[c_len:] * decay
            tm = _tri_inv(low, 6)
            eg = jnp.exp(gcol)
            uw = _bdot(tm, jnp.concatenate([v * b, kb * eg], axis=1))
            s = s_ref[h]
            ws = _bdot(jnp.concatenate([uw[:, dh:], q * eg], axis=0), s)
            v_new = uw[:, :dh] - ws[:c_len]
            o = ws[c_len:] + _bdot(qk, v_new)
            g_last = gcol[c_len - 1:c_len, :]
            s_ref[h] = s * jnp.exp(g_last) + _bdot_tn(k * jnp.exp(g_last - gcol), v_new)
            zc = p_ref[r0:r0 + c_len, 3 * w + l0:3 * w + l0 + dh]
            o_ref[r0:r0 + c_len, l0:l0 + dh] = (_rms(o, nw_ref[...]) * _silu(zc)).astype(o_ref.dtype)


def gated_deltanet(p, conv_w, a_log, dt_bias, norm_w, batch, seq):
    t = p.shape[0]
    nt = seq // SEQ_TILE
    zeros = jnp.zeros((GDN_HEADS,), F32)
    tail = jnp.zeros((LANES - 2 * GDN_HEADS,), F32)
    bias = jnp.concatenate([zeros, dt_bias.astype(F32), tail])[None, :]
    a_row = jnp.concatenate([zeros, -jnp.exp(a_log.astype(F32)), tail])[None, :]
    params = (conv_w, bias, a_row, norm_w[None, :])
    return pl.pallas_call(
        _gdn_kernel,
        grid=(batch, nt),
        in_specs=[pl.BlockSpec((SEQ_TILE, GDN_P), lambda b, i: (b * nt + i, 0))]
                 + [_resident(v.shape) for v in params],
        out_specs=pl.BlockSpec((SEQ_TILE, GDN_WIDTH), lambda b, i: (b * nt + i, 0)),
        out_shape=jax.ShapeDtypeStruct((t, GDN_WIDTH), BF16),
        scratch_shapes=[pltpu.VMEM((PAD_ROWS + SEQ_TILE, 3 * GDN_WIDTH), F32),
                        pltpu.VMEM((GDN_HEADS, GDN_HEAD, GDN_HEAD), F32)],
        compiler_params=_cparams("parallel", "arbitrary"),
        name="gated_deltanet",
    )(p, *params)


def _gdn_weights(w_in_l):
    w = w_in_l[:, OFF_GDN:OFF_SSD]
    qkv = w[:, :3 * GDN_WIDTH]
    ba = w[:, 3 * GDN_WIDTH:3 * GDN_WIDTH + 2 * GDN_HEADS]
    z = w[:, 3 * GDN_WIDTH + 2 * GDN_HEADS:]
    return jnp.concatenate([qkv, z, ba, jnp.zeros((D_MODEL, LANES - 2 * GDN_HEADS), F32)],
                           axis=1).astype(BF16)


RWKV_XG_P = 2 * LANES


def _rwkv_kernel(p_ref, mu_ref, w0_ref, w2_ref, a0_ref, a2_ref, g2_ref, kk_ref, ka_ref, rk_ref,
                 lnw_ref, lnb_ref, o_ref, buf_ref, y_ref, s_ref):
    ts = SEQ_TILE
    c_len = CHUNK
    w = RWKV_WIDTH
    n = RWKV_HEAD

    @pl.when(pl.program_id(1) == 0)
    def _():
        buf_ref[0:PAD_ROWS, :] = jnp.zeros((PAD_ROWS, RWKV_P), F32)
        s_ref[...] = jnp.zeros(s_ref.shape, F32)

    cur = p_ref[...]
    buf_ref[PAD_ROWS:PAD_ROWS + ts, :] = cur
    prev = buf_ref[PAD_ROWS - 1:PAD_ROWS - 1 + ts, :]
    buf_ref[0:PAD_ROWS, :] = buf_ref[ts:ts + PAD_ROWS, :]
    p = cur + (prev - cur) * mu_ref[...]
    r = p[:, :w]
    k = p[:, w:2 * w]
    v = p[:, 2 * w:3 * w]
    lw = p[:, 3 * w:3 * w + LANES]
    xg = p[:, 3 * w + LANES:]

    seg_ones = jnp.where(_iota2((w, w), 0) // n == _iota2((w, w), 1) // n, 1.0, 0.0).astype(BF16)
    seg_sum = lambda t: _dot_split(t, seg_ones)

    wl = w0_ref[...] + _bdot(jnp.tanh(lw), w2_ref[...])
    log_w = -jnp.exp(-_softplus(-wl) - 0.5)
    a = _sigmoid(a0_ref[...] + _bdot(lw, a2_ref[...]))
    gate = _bdot(_sigmoid(xg), g2_ref[...])
    kkr = k * kk_ref[...]
    kk = kkr * lax.rsqrt(seg_sum(kkr * kkr) + EPS)
    k_mod = k * (1.0 + (a - 1.0) * ka_ref[...])
    kka = kk * a

    cs = _split_dot(_chunk_tri(ts), log_w)
    p_inv = jnp.exp(-cs)
    rt = r * jnp.exp(cs)
    kkt = kk * jnp.exp(cs - log_w)
    kh = k_mod * p_inv
    kah = kka * p_inv

    lane = _iota2((1, LANES), 1)
    m0 = jnp.where(lane < n, 1.0, 0.0)
    m1 = 1.0 - m0
    row = _iota2((c_len, LANES), 0)
    col = _iota2((c_len, LANES), 1) % c_len
    strict2 = col < row
    incl2 = col <= row
    same_head = (_iota2((LANES, LANES), 0) // n) == (_iota2((LANES, LANES), 1) // n)
    halves = lambda t: jnp.concatenate([t * m0, t * m1], axis=0)

    for c in range(ts // c_len):
        r0 = c * c_len
        for pr in range(RWKV_HEADS // 2):
            l0 = pr * LANES
            sl = lambda t: t[r0:r0 + c_len, l0:l0 + LANES]
            p_last = jnp.exp(cs[r0 + c_len - 1:r0 + c_len, l0:l0 + LANES])
            kkt_c, rt_c, kh_c, kah_c, v_c = sl(kkt), sl(rt), sl(kh), sl(kah), sl(v)
            lhs = jnp.concatenate([kkt_c, rt_c], axis=0)
            rr = _bdot_nt(lhs, jnp.concatenate([halves(kh_c), halves(kah_c)], axis=0))
            lk2 = jnp.where(strict2, rr[:c_len, :LANES], 0.0)
            la2 = jnp.where(strict2, rr[:c_len, LANES:], 0.0)
            mqk2 = jnp.where(incl2, rr[c_len:, :LANES], 0.0)
            mqa2 = jnp.where(incl2, rr[c_len:, LANES:], 0.0)
            tbd = _tri_inv(halves(la2), 6)
            s = s_ref[pr]
            xa = _bdot_nt(lhs, s)
            vs = halves(v_c)
            rhs = xa[:c_len] + _bdot(lk2, vs)
            tu = _bdot(tbd, halves(rhs))
            u = tu[:c_len] + tu[c_len:]
            us = halves(u)
            o = xa[c_len:] + _bdot(jnp.concatenate([mqk2, -mqa2], axis=1),
                                   jnp.concatenate([vs, us], axis=0))
            upd = _bdot_tn(jnp.concatenate([v_c, u], axis=0),
                           jnp.concatenate([kh_c * p_last, -kah_c * p_last], axis=0))
            s_ref[pr] = s * p_last + jnp.where(same_head, upd, 0.0)
            y_ref[r0:r0 + c_len, l0:l0 + LANES] = o

    o = y_ref[...]
    mean = seg_sum(o) * (1.0 / n)
    d = o - mean
    var = seg_sum(d * d) * (1.0 / n)
    o = d * lax.rsqrt(var + RWKV_GN_EPS) * lnw_ref[...] + lnb_ref[...]
    o = o + seg_sum(r * k_mod * rk_ref[...]) * v
    o_ref[...] = (o * gate).astype(o_ref.dtype)


def rwkv7_time_mix(p, mu, w0, w2, a0, a2, g2, k_k, k_a, r_k, ln_w, ln_b, batch, seq):
    t = p.shape[0]
    nt = seq // SEQ_TILE
    w3 = 3 * RWKV_WIDTH
    o4 = w3 + RWKV_DECAY_LORA
    o5 = o4 + RWKV_A_LORA
    row = lambda v: v.reshape(1, -1).astype(F32)
    mu_p = jnp.concatenate([mu, jnp.zeros((RWKV_XG_P - RWKV_GATE_LORA,), F32)])[None, :]
    zl = jnp.zeros((RWKV_DECAY_LORA, RWKV_WIDTH), F32)
    w2_p = jnp.concatenate([w2, zl], axis=0).astype(BF16)
    a2_p = jnp.concatenate([zl, a2], axis=0).astype(BF16)
    g2_p = jnp.concatenate([g2, jnp.zeros((RWKV_XG_P - RWKV_GATE_LORA, RWKV_WIDTH), F32)],
                           axis=0).astype(BF16)
    params = (mu_p, row(w0), w2_p, row(a0), a2_p, g2_p, row(k_k), row(k_a), row(r_k), row(ln_w), row(ln_b))
    return pl.pallas_call(
        _rwkv_kernel,
        grid=(batch, nt),
        in_specs=[pl.BlockSpec((SEQ_TILE, RWKV_P), lambda b, i: (b * nt + i, 0))]
                 + [_resident(v.shape) for v in params],
        out_specs=pl.BlockSpec((SEQ_TILE, RWKV_WIDTH), lambda b, i: (b * nt + i, 0)),
        out_shape=jax.ShapeDtypeStruct((t, RWKV_WIDTH), BF16),
        scratch_shapes=[pltpu.VMEM((PAD_ROWS + SEQ_TILE, RWKV_P), F32),
                        pltpu.VMEM((SEQ_TILE, RWKV_WIDTH), F32),
                        pltpu.VMEM((RWKV_HEADS // 2, LANES, LANES), F32)],
        compiler_params=_cparams("parallel", "arbitrary"),
        name="rwkv7_time_mix",
    )(p, *params)


def _rwkv_weights(w_in_l):
    w = w_in_l[:, OFF_RWKV:OFF_GDN]
    return jnp.concatenate([w, jnp.zeros((D_MODEL, RWKV_XG_P - RWKV_GATE_LORA), F32)], axis=1).astype(BF16)


MOE_FF_TILE = FF_EXPERT // 4


def _router_kernel(x_ref, g_ref, wr_ref, h_ref, idx_ref, wt_ref):
    h = _rms(x_ref[...], g_ref[...])
    h_ref[...] = h
    wr = wr_ref[...]
    h_hi = h.astype(BF16)
    h_lo = (h - h_hi.astype(F32)).astype(BF16)
    w_hi = wr.astype(BF16)
    w_lo = (wr - w_hi.astype(F32)).astype(BF16)
    logits = (jnp.dot(h_hi, w_hi, preferred_element_type=F32)
              + jnp.dot(h_lo, w_hi, preferred_element_type=F32)
              + jnp.dot(h_hi, w_lo, preferred_element_type=F32))
    lane = _iota2(logits.shape, 1)
    logits = jnp.where(lane < N_EXPERTS, logits, -jnp.inf)
    m1 = jnp.max(logits, axis=-1, keepdims=True)
    i1 = jnp.min(jnp.where(logits == m1, lane, LANES), axis=-1, keepdims=True)
    rest = jnp.where(lane == i1, -jnp.inf, logits)
    m2 = jnp.max(rest, axis=-1, keepdims=True)
    i2 = jnp.min(jnp.where(rest == m2, lane, LANES), axis=-1, keepdims=True)
    e = jnp.exp(m2 - m1)
    w1 = 1.0 / (1.0 + e)
    idx_ref[...] = jnp.where(lane == 0, i1, i2)
    wt_ref[...] = jnp.where(lane == 0, w1, e * w1)


def moe_route(x, g, router):
    t = x.shape[0]
    wr = jnp.concatenate([router.astype(F32), jnp.zeros((D_MODEL, LANES - N_EXPERTS), F32)], axis=1)
    return pl.pallas_call(
        _router_kernel,
        grid=(t // ROW_TILE,),
        in_specs=[_rows(ROW_TILE, D_MODEL), _resident((1, D_MODEL)), _resident(wr.shape)],
        out_specs=[_rows(ROW_TILE, D_MODEL), _rows(ROW_TILE, LANES), _rows(ROW_TILE, LANES)],
        out_shape=[jax.ShapeDtypeStruct((t, D_MODEL), F32), jax.ShapeDtypeStruct((t, LANES), jnp.int32),
                   jax.ShapeDtypeStruct((t, LANES), F32)],
        compiler_params=_cparams("parallel"),
        name="moe_router",
    )(x, g, wr)


def _moe_plan(idx, wts):
    t = idx.shape[0]
    n_pairs = t * TOP_K
    n_blk = -(-(n_pairs + N_EXPERTS * (MOE_BLOCK - 1)) // MOE_BLOCK)
    n_slot = n_blk * MOE_BLOCK
    flat_e = idx[:, :TOP_K].reshape(n_pairs)
    flat_w = wts[:, :TOP_K].reshape(n_pairs)
    onehot = (flat_e[:, None] == jnp.arange(N_EXPERTS, dtype=jnp.int32)[None, :]).astype(jnp.int32)
    rank = jnp.sum((jnp.cumsum(onehot, axis=0) - onehot) * onehot, axis=1)
    counts = jnp.sum(onehot, axis=0)
    padded = (counts + MOE_BLOCK - 1) // MOE_BLOCK * MOE_BLOCK
    pad_end = jnp.cumsum(padded)
    pad_start = pad_end - padded
    dest = pad_start[flat_e] + rank
    pair_id = jnp.arange(n_pairs, dtype=jnp.int32)
    slot_tok = jnp.zeros((n_slot,), jnp.int32).at[dest].set(pair_id // TOP_K)
    slot_dst = jnp.full((n_slot,), n_pairs, jnp.int32).at[dest].set(pair_id)
    slot_w = jnp.zeros((n_slot,), F32).at[dest].set(flat_w)
    blk_start = jnp.arange(n_blk, dtype=jnp.int32) * MOE_BLOCK
    blk_e = jnp.minimum(jnp.sum((pad_end[None, :] <= blk_start[:, None]).astype(jnp.int32), axis=1),
                        N_EXPERTS - 1)
    return slot_tok, slot_dst, slot_w, blk_e


def _moe_kernel(blk_e_ref, tok_ref, tokn_ref, dst_ref, sw_ref, h_hbm, wg_ref, wu_ref, wd_ref, out_hbm,
                xbuf, xb_ref, acc_ref, ybuf, gsem, ssem, *, n_pairs):
    n = pl.program_id(0)
    f = pl.program_id(1)
    n_blk = pl.num_programs(0)
    n_ff = pl.num_programs(1)
    slot = n % 2

    def row_in(tok, r, s):
        return pltpu.make_async_copy(h_hbm.at[pl.ds(tok, 1), :], xbuf.at[s, pl.ds(r, 1), :], gsem.at[s])

    def row_out(r, d):
        return pltpu.make_async_copy(ybuf.at[pl.ds(r, 1), :], out_hbm.at[pl.ds(d, 1), :], ssem.at[0])

    def start_gather(idx_ref, s):
        def body(r, carry):
            row_in(idx_ref[0, 0, r], r, s).start()
            return carry
        lax.fori_loop(0, MOE_BLOCK, body, 0)

    @pl.when(f == 0)
    def _():
        @pl.when(n == 0)
        def _():
            start_gather(tok_ref, 0)

        def wait_body(r, carry):
            row_in(0, r, slot).wait()
            return carry
        lax.fori_loop(0, MOE_BLOCK, wait_body, 0)

        @pl.when(n + 1 < n_blk)
        def _():
            start_gather(tokn_ref, 1 - slot)

        xb_ref[...] = xbuf[slot].astype(BF16)

    xb = xb_ref[...]
    gate = jnp.dot(xb, wg_ref[0], preferred_element_type=F32)
    up = jnp.dot(xb, wu_ref[0], preferred_element_type=F32)
    part = jnp.dot((_silu(gate) * up).astype(BF16), wd_ref[0], preferred_element_type=F32)

    @pl.when(f == 0)
    def _():
        acc_ref[...] = part

    @pl.when(f > 0)
    def _():
        acc_ref[...] += part

    @pl.when(f == n_ff - 1)
    def _():
        ybuf[...] = acc_ref[...] * sw_ref[...]

        def start_body(r, carry):
            d = dst_ref[0, 0, r]

            @pl.when(d < n_pairs)
            def _():
                row_out(r, d).start()
            return carry
        lax.fori_loop(0, MOE_BLOCK, start_body, 0)

        def wait_body(r, carry):
            d = dst_ref[0, 0, r]

            @pl.when(d < n_pairs)
            def _():
                row_out(r, d).wait()
            return carry
        lax.fori_loop(0, MOE_BLOCK, wait_body, 0)


def moe_experts(h, slot_tok, slot_dst, slot_w, blk_e, w_gu, w_down):
    t = h.shape[0]
    n_pairs = t * TOP_K
    n_blk = blk_e.shape[0]
    n_ff = FF_EXPERT // MOE_FF_TILE
    tok3 = slot_tok.reshape(n_blk, 1, MOE_BLOCK)
    dst3 = slot_dst.reshape(n_blk, 1, MOE_BLOCK)
    smem_blk = lambda imap: pl.BlockSpec((1, 1, MOE_BLOCK), imap, memory_space=pltpu.SMEM)
    grid_spec = pltpu.PrefetchScalarGridSpec(
        num_scalar_prefetch=1,
        grid=(n_blk, n_ff),
        in_specs=[
            smem_blk(lambda n, f, be: (n, 0, 0)),
            smem_blk(lambda n, f, be: (jnp.minimum(n + 1, be.shape[0] - 1), 0, 0)),
            smem_blk(lambda n, f, be: (n, 0, 0)),
            pl.BlockSpec((MOE_BLOCK, 1), lambda n, f, be: (n, 0)),
            pl.BlockSpec(memory_space=pl.ANY),
            pl.BlockSpec((1, D_MODEL, MOE_FF_TILE), lambda n, f, be: (be[n], 0, f)),
            pl.BlockSpec((1, D_MODEL, MOE_FF_TILE), lambda n, f, be: (be[n], 0, f + FF_EXPERT // MOE_FF_TILE)),
            pl.BlockSpec((1, MOE_FF_TILE, D_MODEL), lambda n, f, be: (be[n], f, 0)),
        ],
        out_specs=pl.BlockSpec(memory_space=pl.ANY),
        scratch_shapes=[pltpu.VMEM((2, MOE_BLOCK, D_MODEL), F32),
                        pltpu.VMEM((MOE_BLOCK, D_MODEL), BF16),
                        pltpu.VMEM((MOE_BLOCK, D_MODEL), F32),
                        pltpu.VMEM((MOE_BLOCK, D_MODEL), F32),
                        pltpu.SemaphoreType.DMA((2,)),
                        pltpu.SemaphoreType.DMA((1,))],
    )
    return pl.pallas_call(
        functools.partial(_moe_kernel, n_pairs=n_pairs),
        grid_spec=grid_spec,
        out_shape=jax.ShapeDtypeStruct((n_pairs, D_MODEL), F32),
        compiler_params=_cparams("arbitrary", "arbitrary"),
        name="moe_experts",
    )(blk_e, tok3, tok3, dst3, slot_w[:, None], h, w_gu, w_gu, w_down)


def _moe_combine_kernel(x_ref, y_ref, o_ref):
    o_ref[...] = x_ref[...] + y_ref[:, :D_MODEL] + y_ref[:, D_MODEL:]


def moe_combine(x, pair_out):
    t = x.shape[0]
    return pl.pallas_call(
        _moe_combine_kernel,
        grid=(t // ROW_TILE,),
        in_specs=[_rows(ROW_TILE, D_MODEL), _rows(ROW_TILE, TOP_K * D_MODEL)],
        out_specs=_rows(ROW_TILE, D_MODEL),
        out_shape=jax.ShapeDtypeStruct((t, D_MODEL), F32),
        compiler_params=_cparams("parallel"),
        name="moe_combine",
    )(x, pair_out.reshape(t, TOP_K * D_MODEL))


def moe_swiglu(x, g, router, w_gu, w_down):
    h, idx, wts = moe_route(x, g, router)
    slot_tok, slot_dst, slot_w, blk_e = _moe_plan(idx, wts)
    pair_out = moe_experts(h, slot_tok, slot_dst, slot_w, blk_e, w_gu, w_down)
    return moe_combine(x, pair_out)


def kernel(x, positions, norm_mix, w_in, mla_q_norm, mla_kv_norm, mla_w_uq, mla_w_ukv, rwkv_mu, rwkv_w0,
           rwkv_w2, rwkv_a0, rwkv_a2, rwkv_g2, rwkv_k_k, rwkv_k_a, rwkv_r_k, rwkv_ln_w, rwkv_ln_b, gdn_conv,
           gdn_a_log, gdn_dt_bias, gdn_norm, ssd_conv_w, ssd_conv_b, ssd_dt_bias, ssd_a_log, ssd_d, ssd_norm,
           gate_b, w_branch, w_out, norm_ffn, ffn_w_gu, ffn_w_down, moe_router, moe_w_gu, moe_w_down,
           norm_final):
    batch, seq, d = x.shape
    t = batch * seq
    depth = w_in.shape[0]
    xf = x.reshape(t, d)
    cos, sin = rope_tables(positions.reshape(t, 1).astype(jnp.int32))
    row = lambda v: v.reshape(1, -1)
    for layer in range(depth):
        w_l = w_in[layer]
        w_a, wq, wqr, wk, wv = _mla_weights(w_l, mla_w_uq[layer], mla_w_ukv[layer])
        pa, pb, pc, pd = norm_proj(xf, row(norm_mix[layer]),
                                   [w_a, _rwkv_weights(w_l), _gdn_weights(w_l), _ssd_weights(w_l)])
        y_a = mla_attention(pa, cos, sin, row(mla_q_norm[layer]), row(mla_kv_norm[layer]),
                            wq, wqr, wk, wv, batch, seq)
        y_b = rwkv7_time_mix(pb, rwkv_mu[layer], rwkv_w0[layer], rwkv_w2[layer], rwkv_a0[layer],
                             rwkv_a2[layer], rwkv_g2[layer], rwkv_k_k[layer], rwkv_k_a[layer],
                             rwkv_r_k[layer], rwkv_ln_w[layer], rwkv_ln_b[layer], batch, seq)
        y_c = gated_deltanet(pc, gdn_conv[layer], gdn_a_log[layer], gdn_dt_bias[layer], gdn_norm[layer],
                             batch, seq)
        y_d = mamba2_ssd(pd, ssd_conv_w[layer], ssd_conv_b[layer], ssd_dt_bias[layer], ssd_a_log[layer],
                         ssd_d[layer], ssd_norm[layer], batch, seq)
        w_gate = w_l[:, OFF_GATE:].reshape(D_MODEL, N_BRANCH, D_MODEL).transpose(1, 0, 2).astype(BF16)
        xf = merge(xf, row(norm_mix[layer]), w_gate, gate_b[layer], (y_a, y_b, y_c, y_d),
                   w_branch[layer].astype(BF16), w_out[layer].astype(BF16))
        if layer % 2 == 0:
            xf = ffn_dense(xf, row(norm_ffn[layer]), ffn_w_gu[layer // 2].astype(BF16),
                           ffn_w_down[layer // 2].astype(BF16))
        else:
            xf = moe_swiglu(xf, row(norm_ffn[layer]), moe_router[layer // 2],
                            moe_w_gu[layer // 2].astype(BF16), moe_w_down[layer // 2].astype(BF16))
    return final_norm(xf, row(norm_final)).reshape(batch, seq, d)
```

```python
import functools
import math

import jax
import jax.numpy as jnp
from jax import lax
from jax.experimental import pallas as pl
from jax.experimental.pallas import tpu as pltpu

F32 = jnp.float32
BF16 = jnp.bfloat16

D_MODEL = 1024
CHUNK = 64
EPS = 1e-6
MLA_HEADS = 4
MLA_Q_LORA = 384
MLA_KV_LORA = 256
MLA_NOPE = 128
MLA_ROPE = 64
MLA_V = 128
ROPE_THETA = 10000.0
RWKV_HEADS = 8
RWKV_HEAD = 64
RWKV_WIDTH = RWKV_HEADS * RWKV_HEAD
RWKV_DECAY_LORA = 64
RWKV_A_LORA = 64
RWKV_GATE_LORA = 160
RWKV_GN_EPS = 64e-5
GDN_HEADS = 4
GDN_HEAD = 128
GDN_WIDTH = GDN_HEADS * GDN_HEAD
CONV_K = 4
SSD_HEADS = 8
SSD_HEAD = 64
SSD_WIDTH = SSD_HEADS * SSD_HEAD
SSD_GROUPS = 2
SSD_STATE = 128
SSD_CONV_DIM = SSD_WIDTH + 2 * SSD_GROUPS * SSD_STATE
N_BRANCH = 4
BRANCH_WIDTH = 512
FF_DENSE = 2816
N_EXPERTS = 8
TOP_K = 2
FF_EXPERT = 3584
MOE_BLOCK = 512

MLA_COLS = MLA_Q_LORA + MLA_KV_LORA + MLA_ROPE
RWKV_COLS = 3 * RWKV_WIDTH + RWKV_DECAY_LORA + RWKV_A_LORA + RWKV_GATE_LORA
GDN_COLS = 4 * GDN_WIDTH + 2 * GDN_HEADS
SSD_COLS = SSD_WIDTH + SSD_CONV_DIM + SSD_HEADS
OFF_RWKV = MLA_COLS
OFF_GDN = OFF_RWKV + RWKV_COLS
OFF_SSD = OFF_GDN + GDN_COLS
OFF_GATE = OFF_SSD + SSD_COLS

LANES = 128
SUBLANES = 8
VMEM_LIMIT_BYTES = 56 * 2**20

ROW_TILE = 256
SEQ_TILE = 256
ATT_TILE = 256
PAD_ROWS = SUBLANES

MLA_P = MLA_Q_LORA + MLA_KV_LORA + 2 * LANES
RWKV_P = 3 * RWKV_WIDTH + LANES + 2 * LANES
GDN_P = 4 * GDN_WIDTH + LANES
SSD_P = SSD_WIDTH + SSD_CONV_DIM + LANES


def _cparams(*sem):
    return pltpu.CompilerParams(dimension_semantics=sem, vmem_limit_bytes=VMEM_LIMIT_BYTES)


def _resident(shape):
    nd = len(shape)
    return pl.BlockSpec(shape, lambda *_: (0,) * nd, pipeline_mode=pl.Buffered(1))


def _rows(tile, width):
    return pl.BlockSpec((tile, width), lambda i: (i, 0))


def _bdot(a, b):
    return jnp.dot(a.astype(BF16), b.astype(BF16), preferred_element_type=F32)


def _bdot_nt(a, b):
    return lax.dot_general(a.astype(BF16), b.astype(BF16), (((1,), (1,)), ((), ())),
                           preferred_element_type=F32)


def _bdot_tn(a, b):
    return lax.dot_general(a.astype(BF16), b.astype(BF16), (((0,), (0,)), ((), ())),
                           preferred_element_type=F32)


def _split_dot(m01, x):
    hi = x.astype(BF16)
    lo = (x - hi.astype(F32)).astype(BF16)
    return (jnp.dot(m01, hi, preferred_element_type=F32)
            + jnp.dot(m01, lo, preferred_element_type=F32))


def _dot_split(x, m01):
    hi = x.astype(BF16)
    lo = (x - hi.astype(F32)).astype(BF16)
    return (jnp.dot(hi, m01, preferred_element_type=F32)
            + jnp.dot(lo, m01, preferred_element_type=F32))


def _rms(x, g):
    return x * lax.rsqrt(jnp.mean(x * x, axis=-1, keepdims=True) + EPS) * g


def _sigmoid(x):
    return 1.0 / (1.0 + jnp.exp(-x))


def _silu(x):
    return x * _sigmoid(x)


def _softplus(x):
    return jnp.maximum(x, 0.0) + jnp.log(1.0 + jnp.exp(-jnp.abs(x)))


def _iota2(shape, dim):
    return lax.broadcasted_iota(jnp.int32, shape, dim)


def _tri_inv(lows, steps):
    n = lows[0].shape[0]
    eye = (_iota2((n, n), 0) == _iota2((n, n), 1)).astype(F32)
    ps = [eye - low for low in lows]
    qs = [_bdot(low, low) for low in lows]
    for _ in range(steps - 2):
        rs = [_bdot(jnp.concatenate([p, q], axis=0), q) for p, q in zip(ps, qs)]
        ps = [p + r[:n] for p, r in zip(ps, rs)]
        qs = [r[n:] for r in rs]
    return [p + _bdot(p, q) for p, q in zip(ps, qs)]


def _norm_proj_kernel(x_ref, g_ref, wa_ref, wb_ref, wc_ref, wd_ref, oa_ref, ob_ref, oc_ref, od_ref):
    h = _rms(x_ref[...], g_ref[...]).astype(BF16)
    for w_ref, o_ref in ((wa_ref, oa_ref), (wb_ref, ob_ref), (wc_ref, oc_ref), (wd_ref, od_ref)):
        o_ref[...] = jnp.dot(h, w_ref[...], preferred_element_type=F32)


def norm_proj(x, g, weights):
    t = x.shape[0]
    widths = [w.shape[1] for w in weights]
    return pl.pallas_call(
        _norm_proj_kernel,
        grid=(t // ROW_TILE,),
        in_specs=[_rows(ROW_TILE, D_MODEL), _resident((1, D_MODEL))] + [_resident(w.shape) for w in weights],
        out_specs=[_rows(ROW_TILE, n) for n in widths],
        out_shape=[jax.ShapeDtypeStruct((t, n), F32) for n in widths],
        compiler_params=_cparams("parallel"),
        name="norm_proj",
    )(x, g, *weights)


def _rope_table_kernel(pos_ref, freq_ref, cos_ref, sin_ref):
    ang = pos_ref[...].astype(F32) * freq_ref[...]
    cos_ref[...] = jnp.cos(ang)
    sin_ref[...] = jnp.sin(ang)


def rope_tables(pos_col):
    t = pos_col.shape[0]
    half = MLA_ROPE // 2
    inv_freq = ROPE_THETA ** (-jnp.arange(half, dtype=F32) / half)
    freq = jnp.concatenate([inv_freq, inv_freq, jnp.zeros((LANES - MLA_ROPE,), F32)])[None, :]
    tile = min(1024, t)
    return pl.pallas_call(
        _rope_table_kernel,
        grid=(t // tile,),
        in_specs=[_rows(tile, 1), _resident((1, LANES))],
        out_specs=[_rows(tile, LANES), _rows(tile, LANES)],
        out_shape=[jax.ShapeDtypeStruct((t, LANES), F32)] * 2,
        compiler_params=_cparams("parallel"),
        name="rope_tables",
    )(pos_col, freq)


def _mla_prep_kernel(p_ref, cos_ref, sin_ref, qn_ref, kvn_ref, wq_ref, wqr_ref, wk_ref, wv_ref,
                     q_ref, k_ref, v_ref):
    cos = cos_ref[...]
    sin = sin_ref[...]
    scale = (MLA_NOPE + MLA_ROPE) ** -0.5
    nq = _rms(p_ref[:, :MLA_Q_LORA], qn_ref[...]).astype(BF16)
    q = jnp.dot(nq, wq_ref[...], preferred_element_type=F32)
    qr = jnp.dot(nq, wqr_ref[...], preferred_element_type=F32)
    nkv = _rms(p_ref[:, MLA_Q_LORA:MLA_Q_LORA + MLA_KV_LORA], kvn_ref[...]).astype(BF16)
    kn = jnp.dot(nkv, wk_ref[...], preferred_element_type=F32)
    v_ref[...] = jnp.dot(nkv, wv_ref[...], preferred_element_type=F32).astype(BF16)
    o0 = MLA_Q_LORA + MLA_KV_LORA
    k_rope = (p_ref[:, o0:o0 + LANES] * cos + p_ref[:, o0 + LANES:o0 + 2 * LANES] * sin).astype(BF16)
    for h in range(MLA_HEADS):
        c0 = 2 * LANES * h
        q_ref[:, c0:c0 + LANES] = (q[:, c0:c0 + LANES] * scale).astype(BF16)
        q_rope = q[:, c0 + LANES:c0 + 2 * LANES] * cos + qr[:, h * LANES:(h + 1) * LANES] * sin
        q_ref[:, c0 + LANES:c0 + 2 * LANES] = (q_rope * scale).astype(BF16)
        k_ref[:, c0:c0 + LANES] = kn[:, h * LANES:(h + 1) * LANES].astype(BF16)
        k_ref[:, c0 + LANES:c0 + 2 * LANES] = k_rope


def _mla_attn_kernel(q_ref, k_ref, v_ref, o_ref):
    i = pl.program_id(1)
    tq = ATT_TILE
    hq = 2 * LANES
    heads = range(MLA_HEADS)
    qs = [q_ref[:, h * hq:(h + 1) * hq] for h in heads]

    def tile(j, carry, allowed):
        rows = pl.ds(pl.multiple_of(j * ATT_TILE, ATT_TILE), ATT_TILE)
        ss = [lax.dot_general(qs[h], k_ref[rows, h * hq:(h + 1) * hq], (((1,), (1,)), ((), ())),
                              preferred_element_type=F32) for h in heads]
        if allowed is not None:
            ss = [jnp.where(allowed, s, -jnp.inf) for s in ss]
        m_new = [jnp.maximum(carry[h][0], jnp.max(ss[h], axis=-1, keepdims=True)) for h in heads]
        alpha = [jnp.exp(carry[h][0] - m_new[h]) for h in heads]
        ps = [jnp.exp(ss[h] - m_new[h]) for h in heads]
        pv = [jnp.dot(ps[h].astype(BF16), v_ref[rows, h * MLA_V:(h + 1) * MLA_V],
                      preferred_element_type=F32) for h in heads]
        return tuple((m_new[h], carry[h][1] * alpha[h] + jnp.sum(ps[h], axis=-1, keepdims=True),
                      carry[h][2] * alpha[h] + pv[h]) for h in heads)

    init = tuple((jnp.full((tq, 1), -jnp.inf, F32), jnp.zeros((tq, 1), F32), jnp.zeros((tq, MLA_V), F32))
                 for _ in heads)
    carry = lax.fori_loop(0, i, lambda j, c: tile(j, c, None), init)
    allowed = (_iota2((tq, ATT_TILE), 1) // CHUNK) <= (_iota2((tq, ATT_TILE), 0) // CHUNK)
    final = tile(i, carry, allowed)
    for h in heads:
        o_ref[:, h * MLA_V:(h + 1) * MLA_V] = (final[h][2] / final[h][1]).astype(o_ref.dtype)


def mla_attention(p, cos, sin, q_norm, kv_norm, wq, wqr, wk, wv, batch, seq):
    t = p.shape[0]
    hq = 2 * LANES
    q, k, v = pl.pallas_call(
        _mla_prep_kernel,
        grid=(t // ROW_TILE,),
        in_specs=[_rows(ROW_TILE, MLA_P), _rows(ROW_TILE, LANES), _rows(ROW_TILE, LANES),
                  _resident(q_norm.shape), _resident(kv_norm.shape), _resident(wq.shape),
                  _resident(wqr.shape), _resident(wk.shape), _resident(wv.shape)],
        out_specs=[_rows(ROW_TILE, MLA_HEADS * hq), _rows(ROW_TILE, MLA_HEADS * hq),
                   _rows(ROW_TILE, MLA_HEADS * MLA_V)],
        out_shape=[jax.ShapeDtypeStruct((t, MLA_HEADS * hq), BF16),
                   jax.ShapeDtypeStruct((t, MLA_HEADS * hq), BF16),
                   jax.ShapeDtypeStruct((t, MLA_HEADS * MLA_V), BF16)],
        compiler_params=_cparams("parallel"),
        name="mla_prep",
    )(p, cos, sin, q_norm, kv_norm, wq, wqr, wk, wv)
    nq = seq // ATT_TILE
    return pl.pallas_call(
        _mla_attn_kernel,
        grid=(batch, nq),
        in_specs=[pl.BlockSpec((ATT_TILE, MLA_HEADS * hq), lambda b, i: (b * nq + i, 0)),
                  pl.BlockSpec((seq, MLA_HEADS * hq), lambda b, i: (b, 0)),
                  pl.BlockSpec((seq, MLA_HEADS * MLA_V), lambda b, i: (b, 0))],
        out_specs=pl.BlockSpec((ATT_TILE, MLA_HEADS * MLA_V), lambda b, i: (b * nq + i, 0)),
        out_shape=jax.ShapeDtypeStruct((t, MLA_HEADS * MLA_V), BF16),
        compiler_params=_cparams("parallel", "arbitrary"),
        name="mla_attn",
    )(q, k, v)


def _mla_weights(w_in_l, w_uq, w_ukv):
    half = MLA_ROPE // 2
    zpad = lambda rows, n: jnp.zeros((rows, n), F32)
    o0 = MLA_Q_LORA + MLA_KV_LORA
    kr = w_in_l[:, o0:o0 + MLA_ROPE]
    kr_rot = jnp.concatenate([-kr[:, half:], kr[:, :half]], axis=1)
    w_a = jnp.concatenate([w_in_l[:, :o0], kr, zpad(D_MODEL, LANES - MLA_ROPE),
                           kr_rot, zpad(D_MODEL, LANES - MLA_ROPE)], axis=1)
    uq = w_uq.reshape(MLA_Q_LORA, MLA_HEADS, MLA_NOPE + MLA_ROPE)
    rope_w = uq[:, :, MLA_NOPE:]
    wq = jnp.concatenate([uq, jnp.zeros((MLA_Q_LORA, MLA_HEADS, LANES - MLA_ROPE), F32)], axis=2)
    wqr = jnp.concatenate([-rope_w[:, :, half:], rope_w[:, :, :half],
                           jnp.zeros((MLA_Q_LORA, MLA_HEADS, LANES - MLA_ROPE), F32)], axis=2)
    ukv = w_ukv.reshape(MLA_KV_LORA, MLA_HEADS, MLA_NOPE + MLA_V)
    wk = ukv[:, :, :MLA_NOPE].reshape(MLA_KV_LORA, MLA_HEADS * MLA_NOPE)
    wv = ukv[:, :, MLA_NOPE:].reshape(MLA_KV_LORA, MLA_HEADS * MLA_V)
    return (w_a.astype(BF16), wq.reshape(MLA_Q_LORA, -1).astype(BF16),
            wqr.reshape(MLA_Q_LORA, -1).astype(BF16), wk.astype(BF16), wv.astype(BF16))


def _merge_kernel(x_ref, g_ref, wg_ref, gb_ref, ya_ref, yb_ref, yc_ref, yd_ref, wb_ref, wo_ref, o_ref):
    x = x_ref[...]
    h = _rms(x, g_ref[...]).astype(BF16)
    merged = None
    for i, y_ref in enumerate((ya_ref, yb_ref, yc_ref, yd_ref)):
        gate = _sigmoid(jnp.dot(h, wg_ref[i], preferred_element_type=F32) + gb_ref[i:i + 1, :])
        term = gate * jnp.dot(y_ref[...], wb_ref[i], preferred_element_type=F32)
        merged = term if merged is None else merged + term
    o_ref[...] = x + jnp.dot(merged.astype(BF16), wo_ref[...], preferred_element_type=F32)


def merge(x, g, w_gate, gate_b, ys, w_branch, w_out):
    t = x.shape[0]
    return pl.pallas_call(
        _merge_kernel,
        grid=(t // ROW_TILE,),
        in_specs=[_rows(ROW_TILE, D_MODEL), _resident((1, D_MODEL)), _resident(w_gate.shape),
                  _resident(gate_b.shape)] + [_rows(ROW_TILE, BRANCH_WIDTH)] * N_BRANCH
                 + [_resident(w_branch.shape), _resident(w_out.shape)],
        out_specs=_rows(ROW_TILE, D_MODEL),
        out_shape=jax.ShapeDtypeStruct((t, D_MODEL), F32),
        compiler_params=_cparams("parallel"),
        name="merge",
    )(x, g, w_gate, gate_b, *ys, w_branch, w_out)


FF_TILE = FF_DENSE // 2


def _ffn_kernel(x_ref, g_ref, wgu_ref, wd_ref, o_ref):
    x = x_ref[...]
    h = _rms(x, g_ref[...]).astype(BF16)
    acc = x
    for j in range(FF_DENSE // FF_TILE):
        c0 = j * FF_TILE
        gate = jnp.dot(h, wgu_ref[:, c0:c0 + FF_TILE], preferred_element_type=F32)
        up = jnp.dot(h, wgu_ref[:, FF_DENSE + c0:FF_DENSE + c0 + FF_TILE], preferred_element_type=F32)
        act = (_silu(gate) * up).astype(BF16)
        acc = acc + jnp.dot(act, wd_ref[c0:c0 + FF_TILE, :], preferred_element_type=F32)
    o_ref[...] = acc


def ffn_dense(x, g, w_gu, w_down):
    t = x.shape[0]
    return pl.pallas_call(
        _ffn_kernel,
        grid=(t // ROW_TILE,),
        in_specs=[_rows(ROW_TILE, D_MODEL), _resident((1, D_MODEL)), _resident(w_gu.shape),
                  _resident(w_down.shape)],
        out_specs=_rows(ROW_TILE, D_MODEL),
        out_shape=jax.ShapeDtypeStruct((t, D_MODEL), F32),
        compiler_params=_cparams("parallel"),
        name="ffn_dense",
    )(x, g, w_gu, w_down)


def _final_norm_kernel(x_ref, g_ref, o_ref):
    o_ref[...] = _rms(x_ref[...], g_ref[...])


def final_norm(x, g):
    t = x.shape[0]
    tile = min(1024, t)
    return pl.pallas_call(
        _final_norm_kernel,
        grid=(t // tile,),
        in_specs=[_rows(tile, D_MODEL), _resident((1, D_MODEL))],
        out_specs=_rows(tile, D_MODEL),
        out_shape=jax.ShapeDtypeStruct((t, D_MODEL), F32),
        compiler_params=_cparams("parallel"),
        name="final_norm",
    )(x, g)


def _causal_conv(buf_ref, x, w_ref):
    tile = x.shape[0]
    buf_ref[PAD_ROWS:PAD_ROWS + tile, :] = x
    acc = None
    for j in range(CONV_K):
        off = PAD_ROWS - (CONV_K - 1) + j
        term = buf_ref[off:off + tile, :] * w_ref[j:j + 1, :]
        acc = term if acc is None else acc + term
    buf_ref[0:PAD_ROWS, :] = buf_ref[tile:tile + PAD_ROWS, :]
    return acc


def _chunk_tri(tile):
    r = _iota2((tile, tile), 0)
    c = _iota2((tile, tile), 1)
    return jnp.where(c <= r, jnp.where(r // CHUNK == c // CHUNK, 1.0, 0.0), 0.0).astype(BF16)


def _head_expand(width):
    n = LANES * width
    return jnp.where(_iota2((LANES, n), 1) // width == _iota2((LANES, n), 0), 1.0, 0.0).astype(BF16)


def _ssd_kernel(p_ref, cw_ref, cb_ref, dtb_ref, a_ref, d_ref, nw_ref, o_ref, buf_ref, y_ref, h_ref):
    ts = SEQ_TILE
    c_len = CHUNK
    w = SSD_WIDTH
    gn = SSD_GROUPS * SSD_STATE

    @pl.when(pl.program_id(1) == 0)
    def _():
        buf_ref[0:PAD_ROWS, :] = jnp.zeros((PAD_ROWS, SSD_CONV_DIM), F32)
        h_ref[...] = jnp.zeros(h_ref.shape, F32)

    z = p_ref[:, :w]
    xbc = _silu(_causal_conv(buf_ref, p_ref[:, w:w + SSD_CONV_DIM], cw_ref) + cb_ref[...])
    x = xbc[:, :w]
    bm = xbc[:, w:w + gn]
    cm = xbc[:, w + gn:]
    dt = _softplus(p_ref[:, w + SSD_CONV_DIM:] + dtb_ref[...])
    a = dt * a_ref[...]
    acum = _split_dot(_chunk_tri(ts), a)
    expand = _head_expand(SSD_HEAD)[:, :w]
    dt_x = _dot_split(dt, expand)
    acum_x = _dot_split(acum, expand)
    xdt = x * dt_x
    acum_next = pltpu.roll(acum, LANES - 1, axis=1)
    lane = _iota2((c_len, LANES), 1)
    left = lane < SSD_HEAD
    causal2 = (lane % c_len) <= _iota2((c_len, LANES), 0)

    for c in range(ts // c_len):
        r0 = c * c_len
        ac = acum[r0:r0 + c_len]
        at = jnp.concatenate([ac, acum_next[r0:r0 + c_len]], axis=0).T
        ax = acum_x[r0:r0 + c_len]
        a_last = ax[c_len - 1:c_len, :]
        e_in = jnp.exp(ax)
        xdt_c = xdt[r0:r0 + c_len]
        xdt_d = xdt_c * jnp.exp(a_last - ax)
        e_last = jnp.exp(a_last)
        for g in range(SSD_GROUPS):
            bm_g = bm[r0:r0 + c_len, g * SSD_STATE:(g + 1) * SSD_STATE]
            cm_g = cm[r0:r0 + c_len, g * SSD_STATE:(g + 1) * SSD_STATE]
            cb2 = _bdot_nt(cm_g, jnp.concatenate([bm_g, bm_g], axis=0))
            for pp in range(SSD_HEADS // SSD_GROUPS // 2):
                p = g * (SSD_HEADS // SSD_GROUPS // 2) + pp
                l0 = p * LANES
                col2 = jnp.where(left, ac[:, 2 * p:2 * p + 1], ac[:, 2 * p + 1:2 * p + 2])
                row2 = at[2 * p:2 * p + 1, :]
                seg = jnp.exp(jnp.where(causal2, col2 - row2, -jnp.inf))
                xp = xdt_c[:, l0:l0 + LANES]
                xs = jnp.concatenate([jnp.where(left, xp, 0.0), jnp.where(left, 0.0, xp)], axis=0)
                y_diag = _bdot(cb2 * seg, xs)
                hp = h_ref[p]
                y_off = _bdot(cm_g, hp) * e_in[:, l0:l0 + LANES]
                h_ref[p] = hp * e_last[:, l0:l0 + LANES] + _bdot_tn(bm_g, xdt_d[:, l0:l0 + LANES])
                y_ref[r0:r0 + c_len, l0:l0 + LANES] = (
                    y_diag + y_off + d_ref[:, l0:l0 + LANES] * x[r0:r0 + c_len, l0:l0 + LANES])

    y = y_ref[...] * _silu(z)
    gw = w // SSD_GROUPS
    for g in range(SSD_GROUPS):
        yg = y[:, g * gw:(g + 1) * gw]
        o_ref[:, g * gw:(g + 1) * gw] = _rms(yg, nw_ref[:, g * gw:(g + 1) * gw]).astype(o_ref.dtype)


def mamba2_ssd(p, conv_w, conv_b, dt_bias, a_log, d_skip, norm_w, batch, seq):
    t = p.shape[0]
    nt = seq // SEQ_TILE
    pad = lambda v: jnp.concatenate([v.astype(F32), jnp.zeros((LANES - v.shape[0],), F32)])[None, :]
    a_row = pad(-jnp.exp(a_log.astype(F32)))
    d_row = jnp.repeat(d_skip.astype(F32), SSD_HEAD)[None, :]
    params = (conv_w, conv_b[None, :], pad(dt_bias), a_row, d_row, norm_w[None, :])
    return pl.pallas_call(
        _ssd_kernel,
        grid=(batch, nt),
        in_specs=[pl.BlockSpec((SEQ_TILE, SSD_P), lambda b, i: (b * nt + i, 0))]
                 + [_resident(v.shape) for v in params],
        out_specs=pl.BlockSpec((SEQ_TILE, SSD_WIDTH), lambda b, i: (b * nt + i, 0)),
        out_shape=jax.ShapeDtypeStruct((t, SSD_WIDTH), BF16),
        scratch_shapes=[pltpu.VMEM((PAD_ROWS + SEQ_TILE, SSD_CONV_DIM), F32),
                        pltpu.VMEM((SEQ_TILE, SSD_WIDTH), F32),
                        pltpu.VMEM((SSD_HEADS // 2, SSD_STATE, LANES), F32)],
        compiler_params=_cparams("parallel", "arbitrary"),
        name="mamba2_ssd",
    )(p, *params)


def _ssd_weights(w_in_l):
    w = w_in_l[:, OFF_SSD:OFF_GATE]
    return jnp.concatenate([w, jnp.zeros((D_MODEL, LANES - SSD_HEADS), F32)], axis=1).astype(BF16)


def _l2n(x):
    return x * lax.rsqrt(jnp.sum(x * x, axis=-1, keepdims=True) + EPS)


def _gdn_kernel(p_ref, cw_ref, bias_ref, arow_ref, nw_ref, o_ref, buf_ref, s_ref):
    ts = SEQ_TILE
    c_len = CHUNK
    w = GDN_WIDTH
    dh = GDN_HEAD

    @pl.when(pl.program_id(1) == 0)
    def _():
        buf_ref[0:PAD_ROWS, :] = jnp.zeros((PAD_ROWS, 3 * w), F32)
        s_ref[...] = jnp.zeros(s_ref.shape, F32)

    qkv = _silu(_causal_conv(buf_ref, p_ref[:, :3 * w], cw_ref))
    ba = p_ref[:, 4 * w:]
    beta = _sigmoid(ba)
    g = arow_ref[...] * _softplus(ba + bias_ref[...])
    gcum = _split_dot(_chunk_tri(ts), g)
    rr = _iota2((c_len, c_len), 0)
    cc = _iota2((c_len, c_len), 1)
    incl = cc <= rr
    strict = cc < rr

    n_chunks = ts // c_len
    chains = [(c, h) for c in range(n_chunks) for h in range(GDN_HEADS)]
    gts = []
    for c in range(n_chunks):
        gc_all = gcum[c * c_len:(c + 1) * c_len]
        gts.append(jnp.concatenate([gc_all, gc_all], axis=0).T)
    pre = []
    for c, h in chains:
        r0, l0 = c * c_len, h * dh
        q = _l2n(qkv[r0:r0 + c_len, l0:l0 + dh]) * dh ** -0.5
        k = _l2n(qkv[r0:r0 + c_len, w + l0:w + l0 + dh])
        v = qkv[r0:r0 + c_len, 2 * w + l0:2 * w + l0 + dh]
        b = beta[r0:r0 + c_len, h:h + 1]
        gcol = gcum[r0:r0 + c_len, GDN_HEADS + h:GDN_HEADS + h + 1]
        grow = gts[c][GDN_HEADS + h:GDN_HEADS + h + 1, :c_len]
        decay = jnp.exp(jnp.where(incl, gcol - grow, -jnp.inf))
        pre.append((q, k, v, b, gcol, decay, k * b))
    rs = [_bdot_nt(jnp.concatenate([kb, q], axis=0), k) for q, k, v, b, gcol, decay, kb in pre]
    lows = [jnp.where(strict, r[:c_len] * t[5], 0.0) for r, t in zip(rs, pre)]
    qks = [r[c_len:] * t[5] for r, t in zip(rs, pre)]
    tms = _tri_inv(lows, 6)
    egs = [jnp.exp(t[4]) for t in pre]
    uws = [_bdot(tm, jnp.concatenate([v * b, kb * eg], axis=1))
           for tm, eg, (q, k, v, b, gcol, decay, kb) in zip(tms, egs, pre)]

    for c in range(n_chunks):
        r0 = c * c_len
        idx = [c * GDN_HEADS + h for h in range(GDN_HEADS)]
        states = [s_ref[h] for h in range(GDN_HEADS)]
        wss = [_bdot(jnp.concatenate([uws[i][:, dh:], pre[i][0] * egs[i]], axis=0), s)
               for i, s in zip(idx, states)]
        v_news = [uws[i][:, :dh] - ws[:c_len] for i, ws in zip(idx, wss)]
        for h, (i, s, v_new) in enumerate(zip(idx, states, v_news)):
            k, gcol = pre[i][1], pre[i][4]
            g_last = gcol[c_len - 1:c_len, :]
            s_ref[h] = s * jnp.exp(g_last) + _bdot_tn(k * jnp.exp(g_last - gcol), v_new)
        for h, (i, ws, v_new) in enumerate(zip(idx, wss, v_news)):
            l0 = h * dh
            o = ws[c_len:] + _bdot(qks[i], v_new)
            zc = p_ref[r0:r0 + c_len, 3 * w + l0:3 * w + l0 + dh]
            o_ref[r0:r0 + c_len, l0:l0 + dh] = (_rms(o, nw_ref[...]) * _silu(zc)).astype(o_ref.dtype)


def gated_deltanet(p, conv_w, a_log, dt_bias, norm_w, batch, seq):
    t = p.shape[0]
    nt = seq // SEQ_TILE
    zeros = jnp.zeros((GDN_HEADS,), F32)
    tail = jnp.zeros((LANES - 2 * GDN_HEADS,), F32)
    bias = jnp.concatenate([zeros, dt_bias.astype(F32), tail])[None, :]
    a_row = jnp.concatenate([zeros, -jnp.exp(a_log.astype(F32)), tail])[None, :]
    params = (conv_w, bias, a_row, norm_w[None, :])
    return pl.pallas_call(
        _gdn_kernel,
        grid=(batch, nt),
        in_specs=[pl.BlockSpec((SEQ_TILE, GDN_P), lambda b, i: (b * nt + i, 0))]
                 + [_resident(v.shape) for v in params],
        out_specs=pl.BlockSpec((SEQ_TILE, GDN_WIDTH), lambda b, i: (b * nt + i, 0)),
        out_shape=jax.ShapeDtypeStruct((t, GDN_WIDTH), BF16),
        scratch_shapes=[pltpu.VMEM((PAD_ROWS + SEQ_TILE, 3 * GDN_WIDTH), F32),
                        pltpu.VMEM((GDN_HEADS, GDN_HEAD, GDN_HEAD), F32)],
        compiler_params=_cparams("parallel", "arbitrary"),
        name="gated_deltanet",
    )(p, *params)


def _gdn_weights(w_in_l):
    w = w_in_l[:, OFF_GDN:OFF_SSD]
    qkv = w[:, :3 * GDN_WIDTH]
    ba = w[:, 3 * GDN_WIDTH:3 * GDN_WIDTH + 2 * GDN_HEADS]
    z = w[:, 3 * GDN_WIDTH + 2 * GDN_HEADS:]
    return jnp.concatenate([qkv, z, ba, jnp.zeros((D_MODEL, LANES - 2 * GDN_HEADS), F32)],
                           axis=1).astype(BF16)


RWKV_XG_P = 2 * LANES


def _rwkv_kernel(p_ref, mu_ref, w0_ref, w2_ref, a0_ref, a2_ref, g2_ref, kk_ref, ka_ref, rk_ref,
                 lnw_ref, lnb_ref, o_ref, buf_ref, y_ref, s_ref):
    ts = SEQ_TILE
    c_len = CHUNK
    w = RWKV_WIDTH
    n = RWKV_HEAD

    @pl.when(pl.program_id(1) == 0)
    def _():
        buf_ref[0:PAD_ROWS, :] = jnp.zeros((PAD_ROWS, RWKV_P), F32)
        s_ref[...] = jnp.zeros(s_ref.shape, F32)

    cur = p_ref[...]
    buf_ref[PAD_ROWS:PAD_ROWS + ts, :] = cur
    prev = buf_ref[PAD_ROWS - 1:PAD_ROWS - 1 + ts, :]
    buf_ref[0:PAD_ROWS, :] = buf_ref[ts:ts + PAD_ROWS, :]
    p = cur + (prev - cur) * mu_ref[...]
    r = p[:, :w]
    k = p[:, w:2 * w]
    v = p[:, 2 * w:3 * w]
    lw = p[:, 3 * w:3 * w + LANES]
    xg = p[:, 3 * w + LANES:]

    seg_ones = jnp.where(_iota2((w, w), 0) // n == _iota2((w, w), 1) // n, 1.0, 0.0).astype(BF16)
    seg_sum = lambda t: _dot_split(t, seg_ones)

    wl = w0_ref[...] + _bdot(jnp.tanh(lw), w2_ref[...])
    log_w = -jnp.exp(-_softplus(-wl) - 0.5)
    a = _sigmoid(a0_ref[...] + _bdot(lw, a2_ref[...]))
    gate = _bdot(_sigmoid(xg), g2_ref[...])
    kkr = k * kk_ref[...]
    kk = kkr * lax.rsqrt(seg_sum(kkr * kkr) + EPS)
    k_mod = k * (1.0 + (a - 1.0) * ka_ref[...])
    kka = kk * a

    cs = _split_dot(_chunk_tri(ts), log_w)
    p_inv = jnp.exp(-cs)
    rt = r * jnp.exp(cs)
    kkt = kk * jnp.exp(cs - log_w)
    kh = k_mod * p_inv
    kah = kka * p_inv

    lane = _iota2((1, LANES), 1)
    m0 = jnp.where(lane < n, 1.0, 0.0)
    m1 = 1.0 - m0
    row = _iota2((c_len, LANES), 0)
    col = _iota2((c_len, LANES), 1) % c_len
    strict2 = col < row
    incl2 = col <= row
    same_head = (_iota2((LANES, LANES), 0) // n) == (_iota2((LANES, LANES), 1) // n)
    halves = lambda t: jnp.concatenate([t * m0, t * m1], axis=0)

    n_chunks = ts // c_len
    n_pairs = RWKV_HEADS // 2
    chains = [(c, pr) for c in range(n_chunks) for pr in range(n_pairs)]
    pre = []
    for c, pr in chains:
        r0, l0 = c * c_len, pr * LANES
        sl = lambda t: t[r0:r0 + c_len, l0:l0 + LANES]
        p_last = jnp.exp(cs[r0 + c_len - 1:r0 + c_len, l0:l0 + LANES])
        pre.append((sl(kkt), sl(rt), sl(kh), sl(kah), sl(v), p_last))
    rrs = [_bdot_nt(jnp.concatenate([kkt_c, rt_c], axis=0),
                    jnp.concatenate([halves(kh_c), halves(kah_c)], axis=0))
           for kkt_c, rt_c, kh_c, kah_c, v_c, p_last in pre]
    lk2s = [jnp.where(strict2, rr[:c_len, :LANES], 0.0) for rr in rrs]
    la2s = [jnp.where(strict2, rr[:c_len, LANES:], 0.0) for rr in rrs]
    mq2s = [jnp.concatenate([jnp.where(incl2, rr[c_len:, :LANES], 0.0),
                             -jnp.where(incl2, rr[c_len:, LANES:], 0.0)], axis=1) for rr in rrs]
    vss = [halves(t[4]) for t in pre]
    lkvs = [_bdot(lk2, vs) for lk2, vs in zip(lk2s, vss)]
    tbds = _tri_inv([halves(la2) for la2 in la2s], 6)
    tws = [_bdot(tbd, jnp.concatenate([halves(t[0]), halves(lkv)], axis=1))
           for tbd, t, lkv in zip(tbds, pre, lkvs)]
    wu0 = [tw[:c_len] + tw[c_len:] for tw in tws]

    for c in range(n_chunks):
        r0 = c * c_len
        idx = [c * n_pairs + pr for pr in range(n_pairs)]
        states = [s_ref[pr] for pr in range(n_pairs)]
        xas = [_bdot_nt(jnp.concatenate([wu0[i][:, :LANES], pre[i][1]], axis=0), s)
               for i, s in zip(idx, states)]
        us = [xa[:c_len] + wu0[i][:, LANES:] for i, xa in zip(idx, xas)]
        for pr, (i, s, u) in enumerate(zip(idx, states, us)):
            kkt_c, rt_c, kh_c, kah_c, v_c, p_last = pre[i]
            upd = _bdot_tn(jnp.concatenate([v_c, u], axis=0),
                           jnp.concatenate([kh_c * p_last, -kah_c * p_last], axis=0))
            s_ref[pr] = s * p_last + jnp.where(same_head, upd, 0.0)
        for pr, (i, xa, u) in enumerate(zip(idx, xas, us)):
            l0 = pr * LANES
            y_ref[r0:r0 + c_len, l0:l0 + LANES] = xa[c_len:] + _bdot(
                mq2s[i], jnp.concatenate([vss[i], halves(u)], axis=0))

    o = y_ref[...]
    mean = seg_sum(o) * (1.0 / n)
    d = o - mean
    var = seg_sum(d * d) * (1.0 / n)
    o = d * lax.rsqrt(var + RWKV_GN_EPS) * lnw_ref[...] + lnb_ref[...]
    o = o + seg_sum(r * k_mod * rk_ref[...]) * v
    o_ref[...] = (o * gate).astype(o_ref.dtype)


def rwkv7_time_mix(p, mu, w0, w2, a0, a2, g2, k_k, k_a, r_k, ln_w, ln_b, batch, seq):
    t = p.shape[0]
    nt = seq // SEQ_TILE
    w3 = 3 * RWKV_WIDTH
    o4 = w3 + RWKV_DECAY_LORA
    o5 = o4 + RWKV_A_LORA
    row = lambda v: v.reshape(1, -1).astype(F32)
    mu_p = jnp.concatenate([mu, jnp.zeros((RWKV_XG_P - RWKV_GATE_LORA,), F32)])[None, :]
    zl = jnp.zeros((RWKV_DECAY_LORA, RWKV_WIDTH), F32)
    w2_p = jnp.concatenate([w2, zl], axis=0).astype(BF16)
    a2_p = jnp.concatenate([zl, a2], axis=0).astype(BF16)
    g2_p = jnp.concatenate([g2, jnp.zeros((RWKV_XG_P - RWKV_GATE_LORA, RWKV_WIDTH), F32)],
                           axis=0).astype(BF16)
    params = (mu_p, row(w0), w2_p, row(a0), a2_p, g2_p, row(k_k), row(k_a), row(r_k), row(ln_w), row(ln_b))
    return pl.pallas_call(
        _rwkv_kernel,
        grid=(batch, nt),
        in_specs=[pl.BlockSpec((SEQ_TILE, RWKV_P), lambda b, i: (b * nt + i, 0))]
                 + [_resident(v.shape) for v in params],
        out_specs=pl.BlockSpec((SEQ_TILE, RWKV_WIDTH), lambda b, i: (b * nt + i, 0)),
        out_shape=jax.ShapeDtypeStruct((t, RWKV_WIDTH), BF16),
        scratch_shapes=[pltpu.VMEM((PAD_ROWS + SEQ_TILE, RWKV_P), F32),
                        pltpu.VMEM((SEQ_TILE, RWKV_WIDTH), F32),
                        pltpu.VMEM((RWKV_HEADS // 2, LANES, LANES), F32)],
        compiler_params=_cparams("parallel", "arbitrary"),
        name="rwkv7_time_mix",
    )(p, *params)


def _rwkv_weights(w_in_l):
    w = w_in_l[:, OFF_RWKV:OFF_GDN]
    return jnp.concatenate([w, jnp.zeros((D_MODEL, RWKV_XG_P - RWKV_GATE_LORA), F32)], axis=1).astype(BF16)


MOE_FF_TILE = FF_EXPERT // 4


def _router_kernel(x_ref, g_ref, wr_ref, h_ref, idx_ref, wt_ref):
    h = _rms(x_ref[...], g_ref[...])
    h_ref[...] = h
    wr = wr_ref[...]
    h_hi = h.astype(BF16)
    h_lo = (h - h_hi.astype(F32)).astype(BF16)
    w_hi = wr.astype(BF16)
    w_lo = (wr - w_hi.astype(F32)).astype(BF16)
    logits = (jnp.dot(h_hi, w_hi, preferred_element_type=F32)
              + jnp.dot(h_lo, w_hi, preferred_element_type=F32)
              + jnp.dot(h_hi, w_lo, preferred_element_type=F32))
    lane = _iota2(logits.shape, 1)
    logits = jnp.where(lane < N_EXPERTS, logits, -jnp.inf)
    m1 = jnp.max(logits, axis=-1, keepdims=True)
    i1 = jnp.min(jnp.where(logits == m1, lane, LANES), axis=-1, keepdims=True)
    rest = jnp.where(lane == i1, -jnp.inf, logits)
    m2 = jnp.max(rest, axis=-1, keepdims=True)
    i2 = jnp.min(jnp.where(rest == m2, lane, LANES), axis=-1, keepdims=True)
    e = jnp.exp(m2 - m1)
    w1 = 1.0 / (1.0 + e)
    idx_ref[...] = jnp.where(lane == 0, i1, i2)
    wt_ref[...] = jnp.where(lane == 0, w1, e * w1)


def moe_route(x, g, router):
    t = x.shape[0]
    wr = jnp.concatenate([router.astype(F32), jnp.zeros((D_MODEL, LANES - N_EXPERTS), F32)], axis=1)
    return pl.pallas_call(
        _router_kernel,
        grid=(t // ROW_TILE,),
        in_specs=[_rows(ROW_TILE, D_MODEL), _resident((1, D_MODEL)), _resident(wr.shape)],
        out_specs=[_rows(ROW_TILE, D_MODEL), _rows(ROW_TILE, LANES), _rows(ROW_TILE, LANES)],
        out_shape=[jax.ShapeDtypeStruct((t, D_MODEL), F32), jax.ShapeDtypeStruct((t, LANES), jnp.int32),
                   jax.ShapeDtypeStruct((t, LANES), F32)],
        compiler_params=_cparams("parallel"),
        name="moe_router",
    )(x, g, wr)


DMA_UNROLL = 8


def _moe_plan(idx):
    t = idx.shape[0]
    n_pairs = t * TOP_K
    n_blk = -(-(n_pairs + N_EXPERTS * (MOE_BLOCK - 1)) // MOE_BLOCK)
    n_slot = n_blk * MOE_BLOCK
    flat_e = idx[:, :TOP_K].reshape(n_pairs)
    onehot = (flat_e[:, None] == jnp.arange(N_EXPERTS, dtype=jnp.int32)[None, :]).astype(jnp.int32)
    rank = jnp.sum((jnp.cumsum(onehot, axis=0) - onehot) * onehot, axis=1)
    counts = jnp.sum(onehot, axis=0)
    padded = (counts + MOE_BLOCK - 1) // MOE_BLOCK * MOE_BLOCK
    pad_end = jnp.cumsum(padded)
    pad_start = pad_end - padded
    dest = pad_start[flat_e] + rank
    pair_id = jnp.arange(n_pairs, dtype=jnp.int32)
    pair_row = (pair_id % TOP_K) * t + pair_id // TOP_K
    spare = n_pairs + jnp.arange(n_slot, dtype=jnp.int32) % MOE_BLOCK
    slot_dst = spare.at[dest].set(pair_row)
    slot_tok = jnp.where(slot_dst < n_pairs, slot_dst % t, 0)
    blk_start = jnp.arange(n_blk, dtype=jnp.int32) * MOE_BLOCK
    blk_e = jnp.minimum(jnp.sum((pad_end[None, :] <= blk_start[:, None]).astype(jnp.int32), axis=1),
                        N_EXPERTS - 1)
    return slot_tok, slot_dst, blk_e


def _moe_kernel(blk_e_ref, tok_ref, tokn_ref, dst_ref, dstp_ref, h_hbm, wg_ref, wu_ref, wd_ref, out_hbm,
                xbuf, xb_ref, acc_ref, ybuf, gsem, ssem):
    n = pl.program_id(0)
    f = pl.program_id(1)
    n_blk = pl.num_programs(0)
    n_ff = pl.num_programs(1)
    slot = n % 2

    def row_in(tok, r, s):
        return pltpu.make_async_copy(h_hbm.at[pl.ds(tok, 1), :], xbuf.at[s, pl.ds(r, 1), :], gsem.at[s])

    def row_out(r, d):
        return pltpu.make_async_copy(ybuf.at[pl.ds(r, 1), :], out_hbm.at[pl.ds(d, 1), :], ssem.at[0])

    def for_rows(fn):
        def body(r, carry):
            fn(r)
            return carry
        lax.fori_loop(0, MOE_BLOCK, body, 0, unroll=DMA_UNROLL)

    @pl.when(f == 0)
    def _():
        @pl.when(n == 0)
        def _():
            for_rows(lambda r: row_in(tok_ref[0, 0, r], r, 0).start())

        for_rows(lambda r: row_in(tok_ref[0, 0, r], r, slot).wait())

        @pl.when(n + 1 < n_blk)
        def _():
            for_rows(lambda r: row_in(tokn_ref[0, 0, r], r, 1 - slot).start())

        xb_ref[...] = xbuf[slot].astype(BF16)

    xb = xb_ref[...]
    gate = jnp.dot(xb, wg_ref[0], preferred_element_type=F32)
    up = jnp.dot(xb, wu_ref[0], preferred_element_type=F32)
    part = jnp.dot((_silu(gate) * up).astype(BF16), wd_ref[0], preferred_element_type=F32)

    @pl.when(f == 0)
    def _():
        acc_ref[...] = part

    @pl.when(jnp.logical_and(f > 0, f < n_ff - 1))
    def _():
        acc_ref[...] += part

    @pl.when(f == n_ff - 1)
    def _():
        @pl.when(n > 0)
        def _():
            for_rows(lambda r: row_out(r, dstp_ref[0, 0, r]).wait())

        ybuf[...] = acc_ref[...] + part
        for_rows(lambda r: row_out(r, dst_ref[0, 0, r]).start())

        @pl.when(n == n_blk - 1)
        def _():
            for_rows(lambda r: row_out(r, dst_ref[0, 0, r]).wait())


def moe_experts(h, slot_tok, slot_dst, blk_e, w_gu, w_down):
    t = h.shape[0]
    n_blk = blk_e.shape[0]
    n_ff = FF_EXPERT // MOE_FF_TILE
    tok3 = slot_tok.reshape(n_blk, 1, MOE_BLOCK)
    dst3 = slot_dst.reshape(n_blk, 1, MOE_BLOCK)
    smem_blk = lambda imap: pl.BlockSpec((1, 1, MOE_BLOCK), imap, memory_space=pltpu.SMEM)
    grid_spec = pltpu.PrefetchScalarGridSpec(
        num_scalar_prefetch=1,
        grid=(n_blk, n_ff),
        in_specs=[
            smem_blk(lambda n, f, be: (n, 0, 0)),
            smem_blk(lambda n, f, be: (jnp.minimum(n + 1, be.shape[0] - 1), 0, 0)),
            smem_blk(lambda n, f, be: (n, 0, 0)),
            smem_blk(lambda n, f, be: (jnp.maximum(n - 1, 0), 0, 0)),
            pl.BlockSpec(memory_space=pl.ANY),
            pl.BlockSpec((1, D_MODEL, MOE_FF_TILE), lambda n, f, be: (be[n], 0, f)),
            pl.BlockSpec((1, D_MODEL, MOE_FF_TILE), lambda n, f, be: (be[n], 0, f + FF_EXPERT // MOE_FF_TILE)),
            pl.BlockSpec((1, MOE_FF_TILE, D_MODEL), lambda n, f, be: (be[n], f, 0)),
        ],
        out_specs=pl.BlockSpec(memory_space=pl.ANY),
        scratch_shapes=[pltpu.VMEM((2, MOE_BLOCK, D_MODEL), F32),
                        pltpu.VMEM((MOE_BLOCK, D_MODEL), BF16),
                        pltpu.VMEM((MOE_BLOCK, D_MODEL), F32),
                        pltpu.VMEM((MOE_BLOCK, D_MODEL), F32),
                        pltpu.SemaphoreType.DMA((2,)),
                        pltpu.SemaphoreType.DMA((1,))],
    )
    return pl.pallas_call(
        _moe_kernel,
        grid_spec=grid_spec,
        out_shape=jax.ShapeDtypeStruct((TOP_K * t + MOE_BLOCK, D_MODEL), F32),
        compiler_params=_cparams("arbitrary", "arbitrary"),
        name="moe_experts",
    )(blk_e, tok3, tok3, dst3, dst3, h, w_gu, w_gu, w_down)


def _moe_combine_kernel(x_ref, w_ref, y0_ref, y1_ref, o_ref):
    w = w_ref[...]
    o_ref[...] = x_ref[...] + w[:, 0:1] * y0_ref[...] + w[:, 1:2] * y1_ref[...]


def moe_combine(x, wts, pair_out):
    t = x.shape[0]
    nt = t // ROW_TILE
    return pl.pallas_call(
        _moe_combine_kernel,
        grid=(nt,),
        in_specs=[_rows(ROW_TILE, D_MODEL), _rows(ROW_TILE, LANES), _rows(ROW_TILE, D_MODEL),
                  pl.BlockSpec((ROW_TILE, D_MODEL), lambda i: (i + nt, 0))],
        out_specs=_rows(ROW_TILE, D_MODEL),
        out_shape=jax.ShapeDtypeStruct((t, D_MODEL), F32),
        compiler_params=_cparams("parallel"),
        name="moe_combine",
    )(x, wts, pair_out, pair_out)


def moe_swiglu(x, g, router, w_gu, w_down):
    h, idx, wts = moe_route(x, g, router)
    slot_tok, slot_dst, blk_e = _moe_plan(idx)
    pair_out = moe_experts(h, slot_tok, slot_dst, blk_e, w_gu, w_down)
    return moe_combine(x, wts, pair_out)


def kernel(x, positions, norm_mix, w_in, mla_q_norm, mla_kv_norm, mla_w_uq, mla_w_ukv, rwkv_mu, rwkv_w0,
           rwkv_w2, rwkv_a0, rwkv_a2, rwkv_g2, rwkv_k_k, rwkv_k_a, rwkv_r_k, rwkv_ln_w, rwkv_ln_b, gdn_conv,
           gdn_a_log, gdn_dt_bias, gdn_norm, ssd_conv_w, ssd_conv_b, ssd_dt_bias, ssd_a_log, ssd_d, ssd_norm,
           gate_b, w_branch, w_out, norm_ffn, ffn_w_gu, ffn_w_down, moe_router, moe_w_gu, moe_w_down,
           norm_final):
    batch, seq, d = x.shape
    t = batch * seq
    depth = w_in.shape[0]
    xf = x.reshape(t, d)
    cos, sin = rope_tables(positions.reshape(t, 1).astype(jnp.int32))
    row = lambda v: v.reshape(1, -1)
    for layer in range(depth):
        w_l = w_in[layer]
        w_a, wq, wqr, wk, wv = _mla_weights(w_l, mla_w_uq[layer], mla_w_ukv[layer])
        pa, pb, pc, pd = norm_proj(xf, row(norm_mix[layer]),
                                   [w_a, _rwkv_weights(w_l), _gdn_weights(w_l), _ssd_weights(w_l)])
        y_a = mla_attention(pa, cos, sin, row(mla_q_norm[layer]), row(mla_kv_norm[layer]),
                            wq, wqr, wk, wv, batch, seq)
        y_b = rwkv7_time_mix(pb, rwkv_mu[layer], rwkv_w0[layer], rwkv_w2[layer], rwkv_a0[layer],
                             rwkv_a2[layer], rwkv_g2[layer], rwkv_k_k[layer], rwkv_k_a[layer],
                             rwkv_r_k[layer], rwkv_ln_w[layer], rwkv_ln_b[layer], batch, seq)
        y_c = gated_deltanet(pc, gdn_conv[layer], gdn_a_log[layer], gdn_dt_bias[layer], gdn_norm[layer],
                             batch, seq)
        y_d = mamba2_ssd(pd, ssd_conv_w[layer], ssd_conv_b[layer], ssd_dt_bias[layer], ssd_a_log[layer],
                         ssd_d[layer], ssd_norm[layer], batch, seq)
        w_gate = w_l[:, OFF_GATE:].reshape(D_MODEL, N_BRANCH, D_MODEL).transpose(1, 0, 2).astype(BF16)
        xf = merge(xf, row(norm_mix[layer]), w_gate, gate_b[layer], (y_a, y_b, y_c, y_d),
                   w_branch[layer].astype(BF16), w_out[layer].astype(BF16))
        if layer % 2 == 0:
            xf = ffn_dense(xf, row(norm_ffn[layer]), ffn_w_gu[layer // 2].astype(BF16),
                           ffn_w_down[layer // 2].astype(BF16))
        else:
            xf = moe_swiglu(xf, row(norm_ffn[layer]), moe_router[layer // 2],
                            moe_w_gu[layer // 2].astype(BF16), moe_w_down[layer // 2].astype(BF16))
    return final_norm(xf, row(norm_final)).reshape(batch, seq, d)
```

```python
import functools
import math

import jax
import jax.numpy as jnp
from jax import lax
from jax.experimental import pallas as pl
from jax.experimental.pallas import tpu as pltpu

F32 = jnp.float32
BF16 = jnp.bfloat16

D_MODEL = 1024
CHUNK = 64
EPS = 1e-6
MLA_HEADS = 4
MLA_Q_LORA = 384
MLA_KV_LORA = 256
MLA_NOPE = 128
MLA_ROPE = 64
MLA_V = 128
ROPE_THETA = 10000.0
RWKV_HEADS = 8
RWKV_HEAD = 64
RWKV_WIDTH = RWKV_HEADS * RWKV_HEAD
RWKV_DECAY_LORA = 64
RWKV_A_LORA = 64
RWKV_GATE_LORA = 160
RWKV_GN_EPS = 64e-5
GDN_HEADS = 4
GDN_HEAD = 128
GDN_WIDTH = GDN_HEADS * GDN_HEAD
CONV_K = 4
SSD_HEADS = 8
SSD_HEAD = 64
SSD_WIDTH = SSD_HEADS * SSD_HEAD
SSD_GROUPS = 2
SSD_STATE = 128
SSD_CONV_DIM = SSD_WIDTH + 2 * SSD_GROUPS * SSD_STATE
N_BRANCH = 4
BRANCH_WIDTH = 512
FF_DENSE = 2816
N_EXPERTS = 8
TOP_K = 2
FF_EXPERT = 3584
MOE_BLOCK = 512

MLA_COLS = MLA_Q_LORA + MLA_KV_LORA + MLA_ROPE
RWKV_COLS = 3 * RWKV_WIDTH + RWKV_DECAY_LORA + RWKV_A_LORA + RWKV_GATE_LORA
GDN_COLS = 4 * GDN_WIDTH + 2 * GDN_HEADS
SSD_COLS = SSD_WIDTH + SSD_CONV_DIM + SSD_HEADS
OFF_RWKV = MLA_COLS
OFF_GDN = OFF_RWKV + RWKV_COLS
OFF_SSD = OFF_GDN + GDN_COLS
OFF_GATE = OFF_SSD + SSD_COLS

LANES = 128
SUBLANES = 8
VMEM_LIMIT_BYTES = 56 * 2**20

ROW_TILE = 256
SEQ_TILE = 256
ATT_TILE = 256
PAD_ROWS = SUBLANES

MLA_P = MLA_Q_LORA + MLA_KV_LORA + 2 * LANES
RWKV_P = 3 * RWKV_WIDTH + LANES + 2 * LANES
GDN_P = 4 * GDN_WIDTH + LANES
SSD_P = SSD_WIDTH + SSD_CONV_DIM + LANES


def _cparams(*sem):
    return pltpu.CompilerParams(dimension_semantics=sem, vmem_limit_bytes=VMEM_LIMIT_BYTES)


def _resident(shape):
    nd = len(shape)
    return pl.BlockSpec(shape, lambda *_: (0,) * nd, pipeline_mode=pl.Buffered(1))


def _rows(tile, width):
    return pl.BlockSpec((tile, width), lambda i: (i, 0))


def _bdot(a, b):
    return jnp.dot(a.astype(BF16), b.astype(BF16), preferred_element_type=F32)


def _bdot_nt(a, b):
    return lax.dot_general(a.astype(BF16), b.astype(BF16), (((1,), (1,)), ((), ())),
                           preferred_element_type=F32)


def _bdot_tn(a, b):
    return lax.dot_general(a.astype(BF16), b.astype(BF16), (((0,), (0,)), ((), ())),
                           preferred_element_type=F32)


def _split_dot(m01, x):
    hi = x.astype(BF16)
    lo = (x - hi.astype(F32)).astype(BF16)
    return (jnp.dot(m01, hi, preferred_element_type=F32)
            + jnp.dot(m01, lo, preferred_element_type=F32))


def _dot_split(x, m01):
    hi = x.astype(BF16)
    lo = (x - hi.astype(F32)).astype(BF16)
    return (jnp.dot(hi, m01, preferred_element_type=F32)
            + jnp.dot(lo, m01, preferred_element_type=F32))


def _rms(x, g):
    return x * lax.rsqrt(jnp.mean(x * x, axis=-1, keepdims=True) + EPS) * g


def _sigmoid(x):
    return 1.0 / (1.0 + jnp.exp(-x))


def _silu(x):
    return x * _sigmoid(x)


def _softplus(x):
    return jnp.maximum(x, 0.0) + jnp.log(1.0 + jnp.exp(-jnp.abs(x)))


def _iota2(shape, dim):
    return lax.broadcasted_iota(jnp.int32, shape, dim)


def _tri_inv(lows, steps):
    n = lows[0].shape[0]
    eye = (_iota2((n, n), 0) == _iota2((n, n), 1)).astype(F32)
    ps = [eye - low for low in lows]
    qs = [_bdot(low, low) for low in lows]
    for _ in range(steps - 2):
        rs = [_bdot(jnp.concatenate([p, q], axis=0), q) for p, q in zip(ps, qs)]
        ps = [p + r[:n] for p, r in zip(ps, rs)]
        qs = [r[n:] for r in rs]
    return [p + _bdot(p, q) for p, q in zip(ps, qs)]


def _norm_proj_kernel(x_ref, g_ref, wa_ref, wb_ref, wc_ref, wd_ref, oa_ref, ob_ref, oc_ref, od_ref):
    h = _rms(x_ref[...], g_ref[...]).astype(BF16)
    for w_ref, o_ref in ((wa_ref, oa_ref), (wb_ref, ob_ref), (wc_ref, oc_ref), (wd_ref, od_ref)):
        o_ref[...] = jnp.dot(h, w_ref[...], preferred_element_type=F32)


def norm_proj(x, g, weights):
    t = x.shape[0]
    widths = [w.shape[1] for w in weights]
    return pl.pallas_call(
        _norm_proj_kernel,
        grid=(t // ROW_TILE,),
        in_specs=[_rows(ROW_TILE, D_MODEL), _resident((1, D_MODEL))] + [_resident(w.shape) for w in weights],
        out_specs=[_rows(ROW_TILE, n) for n in widths],
        out_shape=[jax.ShapeDtypeStruct((t, n), F32) for n in widths],
        compiler_params=_cparams("parallel"),
        name="norm_proj",
    )(x, g, *weights)


def _rope_table_kernel(pos_ref, freq_ref, cos_ref, sin_ref):
    ang = pos_ref[...].astype(F32) * freq_ref[...]
    cos_ref[...] = jnp.cos(ang)
    sin_ref[...] = jnp.sin(ang)


def rope_tables(pos_col):
    t = pos_col.shape[0]
    half = MLA_ROPE // 2
    inv_freq = ROPE_THETA ** (-jnp.arange(half, dtype=F32) / half)
    freq = jnp.concatenate([inv_freq, inv_freq, jnp.zeros((LANES - MLA_ROPE,), F32)])[None, :]
    tile = min(1024, t)
    return pl.pallas_call(
        _rope_table_kernel,
        grid=(t // tile,),
        in_specs=[_rows(tile, 1), _resident((1, LANES))],
        out_specs=[_rows(tile, LANES), _rows(tile, LANES)],
        out_shape=[jax.ShapeDtypeStruct((t, LANES), F32)] * 2,
        compiler_params=_cparams("parallel"),
        name="rope_tables",
    )(pos_col, freq)


def _mla_prep_kernel(p_ref, cos_ref, sin_ref, qn_ref, kvn_ref, wq_ref, wqr_ref, wk_ref, wv_ref,
                     q_ref, k_ref, v_ref):
    cos = cos_ref[...]
    sin = sin_ref[...]
    scale = (MLA_NOPE + MLA_ROPE) ** -0.5
    nq = _rms(p_ref[:, :MLA_Q_LORA], qn_ref[...]).astype(BF16)
    q = jnp.dot(nq, wq_ref[...], preferred_element_type=F32)
    qr = jnp.dot(nq, wqr_ref[...], preferred_element_type=F32)
    nkv = _rms(p_ref[:, MLA_Q_LORA:MLA_Q_LORA + MLA_KV_LORA], kvn_ref[...]).astype(BF16)
    kn = jnp.dot(nkv, wk_ref[...], preferred_element_type=F32)
    v_ref[...] = jnp.dot(nkv, wv_ref[...], preferred_element_type=F32).astype(BF16)
    o0 = MLA_Q_LORA + MLA_KV_LORA
    k_rope = (p_ref[:, o0:o0 + LANES] * cos + p_ref[:, o0 + LANES:o0 + 2 * LANES] * sin).astype(BF16)
    for h in range(MLA_HEADS):
        c0 = 2 * LANES * h
        q_ref[:, c0:c0 + LANES] = (q[:, c0:c0 + LANES] * scale).astype(BF16)
        q_rope = q[:, c0 + LANES:c0 + 2 * LANES] * cos + qr[:, h * LANES:(h + 1) * LANES] * sin
        q_ref[:, c0 + LANES:c0 + 2 * LANES] = (q_rope * scale).astype(BF16)
        k_ref[:, c0:c0 + LANES] = kn[:, h * LANES:(h + 1) * LANES].astype(BF16)
        k_ref[:, c0 + LANES:c0 + 2 * LANES] = k_rope


def _mla_attn_kernel(q_ref, k_ref, v_ref, o_ref):
    i = pl.program_id(1)
    tq = ATT_TILE
    hq = 2 * LANES
    heads = range(MLA_HEADS)
    qs = [q_ref[:, h * hq:(h + 1) * hq] for h in heads]

    def tile(j, carry, allowed):
        rows = pl.ds(pl.multiple_of(j * ATT_TILE, ATT_TILE), ATT_TILE)
        ss = [lax.dot_general(qs[h], k_ref[rows, h * hq:(h + 1) * hq], (((1,), (1,)), ((), ())),
                              preferred_element_type=F32) for h in heads]
        if allowed is not None:
            ss = [jnp.where(allowed, s, -jnp.inf) for s in ss]
        m_new = [jnp.maximum(carry[h][0], jnp.max(ss[h], axis=-1, keepdims=True)) for h in heads]
        alpha = [jnp.exp(carry[h][0] - m_new[h]) for h in heads]
        ps = [jnp.exp(ss[h] - m_new[h]) for h in heads]
        pv = [jnp.dot(ps[h].astype(BF16), v_ref[rows, h * MLA_V:(h + 1) * MLA_V],
                      preferred_element_type=F32) for h in heads]
        return tuple((m_new[h], carry[h][1] * alpha[h] + jnp.sum(ps[h], axis=-1, keepdims=True),
                      carry[h][2] * alpha[h] + pv[h]) for h in heads)

    init = tuple((jnp.full((tq, 1), -jnp.inf, F32), jnp.zeros((tq, 1), F32), jnp.zeros((tq, MLA_V), F32))
                 for _ in heads)
    carry = lax.fori_loop(0, i, lambda j, c: tile(j, c, None), init)
    allowed = (_iota2((tq, ATT_TILE), 1) // CHUNK) <= (_iota2((tq, ATT_TILE), 0) // CHUNK)
    final = tile(i, carry, allowed)
    for h in heads:
        o_ref[:, h * MLA_V:(h + 1) * MLA_V] = (final[h][2] / final[h][1]).astype(o_ref.dtype)


def mla_attention(p, cos, sin, q_norm, kv_norm, wq, wqr, wk, wv, batch, seq):
    t = p.shape[0]
    hq = 2 * LANES
    q, k, v = pl.pallas_call(
        _mla_prep_kernel,
        grid=(t // ROW_TILE,),
        in_specs=[_rows(ROW_TILE, MLA_P), _rows(ROW_TILE, LANES), _rows(ROW_TILE, LANES),
                  _resident(q_norm.shape), _resident(kv_norm.shape), _resident(wq.shape),
                  _resident(wqr.shape), _resident(wk.shape), _resident(wv.shape)],
        out_specs=[_rows(ROW_TILE, MLA_HEADS * hq), _rows(ROW_TILE, MLA_HEADS * hq),
                   _rows(ROW_TILE, MLA_HEADS * MLA_V)],
        out_shape=[jax.ShapeDtypeStruct((t, MLA_HEADS * hq), BF16),
                   jax.ShapeDtypeStruct((t, MLA_HEADS * hq), BF16),
                   jax.ShapeDtypeStruct((t, MLA_HEADS * MLA_V), BF16)],
        compiler_params=_cparams("parallel"),
        name="mla_prep",
    )(p, cos, sin, q_norm, kv_norm, wq, wqr, wk, wv)
    nq = seq // ATT_TILE
    return pl.pallas_call(
        _mla_attn_kernel,
        grid=(batch, nq),
        in_specs=[pl.BlockSpec((ATT_TILE, MLA_HEADS * hq), lambda b, i: (b * nq + i, 0)),
                  pl.BlockSpec((seq, MLA_HEADS * hq), lambda b, i: (b, 0)),
                  pl.BlockSpec((seq, MLA_HEADS * MLA_V), lambda b, i: (b, 0))],
        out_specs=pl.BlockSpec((ATT_TILE, MLA_HEADS * MLA_V), lambda b, i: (b * nq + i, 0)),
        out_shape=jax.ShapeDtypeStruct((t, MLA_HEADS * MLA_V), BF16),
        compiler_params=_cparams("parallel", "arbitrary"),
        name="mla_attn",
    )(q, k, v)


def _mla_weights(w_in_l, w_uq, w_ukv):
    half = MLA_ROPE // 2
    zpad = lambda rows, n: jnp.zeros((rows, n), F32)
    o0 = MLA_Q_LORA + MLA_KV_LORA
    kr = w_in_l[:, o0:o0 + MLA_ROPE]
    kr_rot = jnp.concatenate([-kr[:, half:], kr[:, :half]], axis=1)
    w_a = jnp.concatenate([w_in_l[:, :o0], kr, zpad(D_MODEL, LANES - MLA_ROPE),
                           kr_rot, zpad(D_MODEL, LANES - MLA_ROPE)], axis=1)
    uq = w_uq.reshape(MLA_Q_LORA, MLA_HEADS, MLA_NOPE + MLA_ROPE)
    rope_w = uq[:, :, MLA_NOPE:]
    wq = jnp.concatenate([uq, jnp.zeros((MLA_Q_LORA, MLA_HEADS, LANES - MLA_ROPE), F32)], axis=2)
    wqr = jnp.concatenate([-rope_w[:, :, half:], rope_w[:, :, :half],
                           jnp.zeros((MLA_Q_LORA, MLA_HEADS, LANES - MLA_ROPE), F32)], axis=2)
    ukv = w_ukv.reshape(MLA_KV_LORA, MLA_HEADS, MLA_NOPE + MLA_V)
    wk = ukv[:, :, :MLA_NOPE].reshape(MLA_KV_LORA, MLA_HEADS * MLA_NOPE)
    wv = ukv[:, :, MLA_NOPE:].reshape(MLA_KV_LORA, MLA_HEADS * MLA_V)
    return (w_a.astype(BF16), wq.reshape(MLA_Q_LORA, -1).astype(BF16),
            wqr.reshape(MLA_Q_LORA, -1).astype(BF16), wk.astype(BF16), wv.astype(BF16))


def _merge_kernel(x_ref, g_ref, wg_ref, gb_ref, ya_ref, yb_ref, yc_ref, yd_ref, wb_ref, wo_ref, o_ref):
    x = x_ref[...]
    h = _rms(x, g_ref[...]).astype(BF16)
    merged = None
    for i, y_ref in enumerate((ya_ref, yb_ref, yc_ref, yd_ref)):
        gate = _sigmoid(jnp.dot(h, wg_ref[i], preferred_element_type=F32) + gb_ref[i:i + 1, :])
        term = gate * jnp.dot(y_ref[...], wb_ref[i], preferred_element_type=F32)
        merged = term if merged is None else merged + term
    o_ref[...] = x + jnp.dot(merged.astype(BF16), wo_ref[...], preferred_element_type=F32)


def merge(x, g, w_gate, gate_b, ys, w_branch, w_out):
    t = x.shape[0]
    return pl.pallas_call(
        _merge_kernel,
        grid=(t // ROW_TILE,),
        in_specs=[_rows(ROW_TILE, D_MODEL), _resident((1, D_MODEL)), _resident(w_gate.shape),
                  _resident(gate_b.shape)] + [_rows(ROW_TILE, BRANCH_WIDTH)] * N_BRANCH
                 + [_resident(w_branch.shape), _resident(w_out.shape)],
        out_specs=_rows(ROW_TILE, D_MODEL),
        out_shape=jax.ShapeDtypeStruct((t, D_MODEL), F32),
        compiler_params=_cparams("parallel"),
        name="merge",
    )(x, g, w_gate, gate_b, *ys, w_branch, w_out)


FF_TILE = FF_DENSE // 2


def _ffn_kernel(x_ref, g_ref, wgu_ref, wd_ref, o_ref):
    x = x_ref[...]
    h = _rms(x, g_ref[...]).astype(BF16)
    acc = x
    for j in range(FF_DENSE // FF_TILE):
        c0 = j * FF_TILE
        gate = jnp.dot(h, wgu_ref[:, c0:c0 + FF_TILE], preferred_element_type=F32)
        up = jnp.dot(h, wgu_ref[:, FF_DENSE + c0:FF_DENSE + c0 + FF_TILE], preferred_element_type=F32)
        act = (_silu(gate) * up).astype(BF16)
        acc = acc + jnp.dot(act, wd_ref[c0:c0 + FF_TILE, :], preferred_element_type=F32)
    o_ref[...] = acc


def ffn_dense(x, g, w_gu, w_down):
    t = x.shape[0]
    return pl.pallas_call(
        _ffn_kernel,
        grid=(t // ROW_TILE,),
        in_specs=[_rows(ROW_TILE, D_MODEL), _resident((1, D_MODEL)), _resident(w_gu.shape),
                  _resident(w_down.shape)],
        out_specs=_rows(ROW_TILE, D_MODEL),
        out_shape=jax.ShapeDtypeStruct((t, D_MODEL), F32),
        compiler_params=_cparams("parallel"),
        name="ffn_dense",
    )(x, g, w_gu, w_down)


def _final_norm_kernel(x_ref, g_ref, o_ref):
    o_ref[...] = _rms(x_ref[...], g_ref[...])


def final_norm(x, g):
    t = x.shape[0]
    tile = min(1024, t)
    return pl.pallas_call(
        _final_norm_kernel,
        grid=(t // tile,),
        in_specs=[_rows(tile, D_MODEL), _resident((1, D_MODEL))],
        out_specs=_rows(tile, D_MODEL),
        out_shape=jax.ShapeDtypeStruct((t, D_MODEL), F32),
        compiler_params=_cparams("parallel"),
        name="final_norm",
    )(x, g)


def _causal_conv(buf_ref, x, w_ref):
    tile = x.shape[0]
    buf_ref[PAD_ROWS:PAD_ROWS + tile, :] = x
    acc = None
    for j in range(CONV_K):
        off = PAD_ROWS - (CONV_K - 1) + j
        term = buf_ref[off:off + tile, :] * w_ref[j:j + 1, :]
        acc = term if acc is None else acc + term
    buf_ref[0:PAD_ROWS, :] = buf_ref[tile:tile + PAD_ROWS, :]
    return acc


def _chunk_tri(tile):
    r = _iota2((tile, tile), 0)
    c = _iota2((tile, tile), 1)
    return jnp.where(c <= r, jnp.where(r // CHUNK == c // CHUNK, 1.0, 0.0), 0.0).astype(BF16)


def _head_expand(width):
    n = LANES * width
    return jnp.where(_iota2((LANES, n), 1) // width == _iota2((LANES, n), 0), 1.0, 0.0).astype(BF16)


def _ssd_kernel(p_ref, cw_ref, cb_ref, dtb_ref, a_ref, d_ref, nw_ref, o_ref, buf_ref, y_ref, h_ref):
    ts = SEQ_TILE
    c_len = CHUNK
    w = SSD_WIDTH
    gn = SSD_GROUPS * SSD_STATE

    @pl.when(pl.program_id(1) == 0)
    def _():
        buf_ref[0:PAD_ROWS, :] = jnp.zeros((PAD_ROWS, SSD_CONV_DIM), F32)
        h_ref[...] = jnp.zeros(h_ref.shape, F32)

    z = p_ref[:, :w]
    xbc = _silu(_causal_conv(buf_ref, p_ref[:, w:w + SSD_CONV_DIM], cw_ref) + cb_ref[...])
    x = xbc[:, :w]
    bm = xbc[:, w:w + gn]
    cm = xbc[:, w + gn:]
    dt = _softplus(p_ref[:, w + SSD_CONV_DIM:] + dtb_ref[...])
    a = dt * a_ref[...]
    acum = _split_dot(_chunk_tri(ts), a)
    expand = _head_expand(SSD_HEAD)[:, :w]
    dt_x = _dot_split(dt, expand)
    acum_x = _dot_split(acum, expand)
    xdt = x * dt_x
    acum_next = pltpu.roll(acum, LANES - 1, axis=1)
    lane = _iota2((c_len, LANES), 1)
    left = lane < SSD_HEAD
    causal2 = (lane % c_len) <= _iota2((c_len, LANES), 0)

    for c in range(ts // c_len):
        r0 = c * c_len
        ac = acum[r0:r0 + c_len]
        at = jnp.concatenate([ac, acum_next[r0:r0 + c_len]], axis=0).T
        ax = acum_x[r0:r0 + c_len]
        a_last = ax[c_len - 1:c_len, :]
        e_in = jnp.exp(ax)
        xdt_c = xdt[r0:r0 + c_len]
        xdt_d = xdt_c * jnp.exp(a_last - ax)
        e_last = jnp.exp(a_last)
        for g in range(SSD_GROUPS):
            bm_g = bm[r0:r0 + c_len, g * SSD_STATE:(g + 1) * SSD_STATE]
            cm_g = cm[r0:r0 + c_len, g * SSD_STATE:(g + 1) * SSD_STATE]
            cb2 = _bdot_nt(cm_g, jnp.concatenate([bm_g, bm_g], axis=0))
            for pp in range(SSD_HEADS // SSD_GROUPS // 2):
                p = g * (SSD_HEADS // SSD_GROUPS // 2) + pp
                l0 = p * LANES
                col2 = jnp.where(left, ac[:, 2 * p:2 * p + 1], ac[:, 2 * p + 1:2 * p + 2])
                row2 = at[2 * p:2 * p + 1, :]
                seg = jnp.exp(jnp.where(causal2, col2 - row2, -jnp.inf))
                xp = xdt_c[:, l0:l0 + LANES]
                xs = jnp.concatenate([jnp.where(left, xp, 0.0), jnp.where(left, 0.0, xp)], axis=0)
                y_diag = _bdot(cb2 * seg, xs)
                hp = h_ref[p]
                y_off = _bdot(cm_g, hp) * e_in[:, l0:l0 + LANES]
                h_ref[p] = hp * e_last[:, l0:l0 + LANES] + _bdot_tn(bm_g, xdt_d[:, l0:l0 + LANES])
                y_ref[r0:r0 + c_len, l0:l0 + LANES] = (
                    y_diag + y_off + d_ref[:, l0:l0 + LANES] * x[r0:r0 + c_len, l0:l0 + LANES])

    y = y_ref[...] * _silu(z)
    gw = w // SSD_GROUPS
    for g in range(SSD_GROUPS):
        yg = y[:, g * gw:(g + 1) * gw]
        o_ref[:, g * gw:(g + 1) * gw] = _rms(yg, nw_ref[:, g * gw:(g + 1) * gw]).astype(o_ref.dtype)


def mamba2_ssd(p, conv_w, conv_b, dt_bias, a_log, d_skip, norm_w, batch, seq):
    t = p.shape[0]
    nt = seq // SEQ_TILE
    pad = lambda v: jnp.concatenate([v.astype(F32), jnp.zeros((LANES - v.shape[0],), F32)])[None, :]
    a_row = pad(-jnp.exp(a_log.astype(F32)))
    d_row = jnp.repeat(d_skip.astype(F32), SSD_HEAD)[None, :]
    params = (conv_w, conv_b[None, :], pad(dt_bias), a_row, d_row, norm_w[None, :])
    return pl.pallas_call(
        _ssd_kernel,
        grid=(batch, nt),
        in_specs=[pl.BlockSpec((SEQ_TILE, SSD_P), lambda b, i: (b * nt + i, 0))]
                 + [_resident(v.shape) for v in params],
        out_specs=pl.BlockSpec((SEQ_TILE, SSD_WIDTH), lambda b, i: (b * nt + i, 0)),
        out_shape=jax.ShapeDtypeStruct((t, SSD_WIDTH), BF16),
        scratch_shapes=[pltpu.VMEM((PAD_ROWS + SEQ_TILE, SSD_CONV_DIM), F32),
                        pltpu.VMEM((SEQ_TILE, SSD_WIDTH), F32),
                        pltpu.VMEM((SSD_HEADS // 2, SSD_STATE, LANES), F32)],
        compiler_params=_cparams("parallel", "arbitrary"),
        name="mamba2_ssd",
    )(p, *params)


def _ssd_weights(w_in_l):
    w = w_in_l[:, OFF_SSD:OFF_GATE]
    return jnp.concatenate([w, jnp.zeros((D_MODEL, LANES - SSD_HEADS), F32)], axis=1).astype(BF16)


def _l2n(x):
    return x * lax.rsqrt(jnp.sum(x * x, axis=-1, keepdims=True) + EPS)


def _gdn_kernel(p_ref, cw_ref, bias_ref, arow_ref, nw_ref, o_ref, buf_ref, s_ref):
    ts = SEQ_TILE
    c_len = CHUNK
    w = GDN_WIDTH
    dh = GDN_HEAD

    @pl.when(pl.program_id(1) == 0)
    def _():
        buf_ref[0:PAD_ROWS, :] = jnp.zeros((PAD_ROWS, 3 * w), F32)
        s_ref[...] = jnp.zeros(s_ref.shape, F32)

    qkv = _silu(_causal_conv(buf_ref, p_ref[:, :3 * w], cw_ref))
    ba = p_ref[:, 4 * w:]
    beta = _sigmoid(ba)
    g = arow_ref[...] * _softplus(ba + bias_ref[...])
    gcum = _split_dot(_chunk_tri(ts), g)
    rr = _iota2((c_len, c_len), 0)
    cc = _iota2((c_len, c_len), 1)
    incl = cc <= rr
    strict = cc < rr

    n_chunks = ts // c_len
    chains = [(c, h) for c in range(n_chunks) for h in range(GDN_HEADS)]
    gts = []
    for c in range(n_chunks):
        gc_all = gcum[c * c_len:(c + 1) * c_len]
        gts.append(jnp.concatenate([gc_all, gc_all], axis=0).T)
    pre = []
    for c, h in chains:
        r0, l0 = c * c_len, h * dh
        q = _l2n(qkv[r0:r0 + c_len, l0:l0 + dh]) * dh ** -0.5
        k = _l2n(qkv[r0:r0 + c_len, w + l0:w + l0 + dh])
        v = qkv[r0:r0 + c_len, 2 * w + l0:2 * w + l0 + dh]
        b = beta[r0:r0 + c_len, h:h + 1]
        gcol = gcum[r0:r0 + c_len, GDN_HEADS + h:GDN_HEADS + h + 1]
        grow = gts[c][GDN_HEADS + h:GDN_HEADS + h + 1, :c_len]
        decay = jnp.exp(jnp.where(incl, gcol - grow, -jnp.inf))
        pre.append((q, k, v, b, gcol, decay, k * b))
    rs = [_bdot_nt(jnp.concatenate([kb, q], axis=0), k) for q, k, v, b, gcol, decay, kb in pre]
    lows = [jnp.where(strict, r[:c_len] * t[5], 0.0) for r, t in zip(rs, pre)]
    qks = [r[c_len:] * t[5] for r, t in zip(rs, pre)]
    tms = _tri_inv(lows, 6)
    egs = [jnp.exp(t[4]) for t in pre]
    uws = [_bdot(tm, jnp.concatenate([v * b, kb * eg], axis=1))
           for tm, eg, (q, k, v, b, gcol, decay, kb) in zip(tms, egs, pre)]

    for c in range(n_chunks):
        r0 = c * c_len
        idx = [c * GDN_HEADS + h for h in range(GDN_HEADS)]
        states = [s_ref[h] for h in range(GDN_HEADS)]
        wss = [_bdot(jnp.concatenate([uws[i][:, dh:], pre[i][0] * egs[i]], axis=0), s)
               for i, s in zip(idx, states)]
        v_news = [uws[i][:, :dh] - ws[:c_len] for i, ws in zip(idx, wss)]
        for h, (i, s, v_new) in enumerate(zip(idx, states, v_news)):
            k, gcol = pre[i][1], pre[i][4]
            g_last = gcol[c_len - 1:c_len, :]
            s_ref[h] = s * jnp.exp(g_last) + _bdot_tn(k * jnp.exp(g_last - gcol), v_new)
        for h, (i, ws, v_new) in enumerate(zip(idx, wss, v_news)):
            l0 = h * dh
            o = ws[c_len:] + _bdot(qks[i], v_new)
            zc = p_ref[r0:r0 + c_len, 3 * w + l0:3 * w + l0 + dh]
            o_ref[r0:r0 + c_len, l0:l0 + dh] = (_rms(o, nw_ref[...]) * _silu(zc)).astype(o_ref.dtype)


def gated_deltanet(p, conv_w, a_log, dt_bias, norm_w, batch, seq):
    t = p.shape[0]
    nt = seq // SEQ_TILE
    zeros = jnp.zeros((GDN_HEADS,), F32)
    tail = jnp.zeros((LANES - 2 * GDN_HEADS,), F32)
    bias = jnp.concatenate([zeros, dt_bias.astype(F32), tail])[None, :]
    a_row = jnp.concatenate([zeros, -jnp.exp(a_log.astype(F32)), tail])[None, :]
    params = (conv_w, bias, a_row, norm_w[None, :])
    return pl.pallas_call(
        _gdn_kernel,
        grid=(batch, nt),
        in_specs=[pl.BlockSpec((SEQ_TILE, GDN_P), lambda b, i: (b * nt + i, 0))]
                 + [_resident(v.shape) for v in params],
        out_specs=pl.BlockSpec((SEQ_TILE, GDN_WIDTH), lambda b, i: (b * nt + i, 0)),
        out_shape=jax.ShapeDtypeStruct((t, GDN_WIDTH), BF16),
        scratch_shapes=[pltpu.VMEM((PAD_ROWS + SEQ_TILE, 3 * GDN_WIDTH), F32),
                        pltpu.VMEM((GDN_HEADS, GDN_HEAD, GDN_HEAD), F32)],
        compiler_params=_cparams("parallel", "arbitrary"),
        name="gated_deltanet",
    )(p, *params)


def _gdn_weights(w_in_l):
    w = w_in_l[:, OFF_GDN:OFF_SSD]
    qkv = w[:, :3 * GDN_WIDTH]
    ba = w[:, 3 * GDN_WIDTH:3 * GDN_WIDTH + 2 * GDN_HEADS]
    z = w[:, 3 * GDN_WIDTH + 2 * GDN_HEADS:]
    return jnp.concatenate([qkv, z, ba, jnp.zeros((D_MODEL, LANES - 2 * GDN_HEADS), F32)],
                           axis=1).astype(BF16)


RWKV_XG_P = 2 * LANES


def _rwkv_kernel(p_ref, mu_ref, w0_ref, w2_ref, a0_ref, a2_ref, g2_ref, kk_ref, ka_ref, rk_ref,
                 lnw_ref, lnb_ref, o_ref, buf_ref, y_ref, s_ref):
    ts = SEQ_TILE
    c_len = CHUNK
    w = RWKV_WIDTH
    n = RWKV_HEAD

    @pl.when(pl.program_id(1) == 0)
    def _():
        buf_ref[0:PAD_ROWS, :] = jnp.zeros((PAD_ROWS, RWKV_P), F32)
        s_ref[...] = jnp.zeros(s_ref.shape, F32)

    cur = p_ref[...]
    buf_ref[PAD_ROWS:PAD_ROWS + ts, :] = cur
    prev = buf_ref[PAD_ROWS - 1:PAD_ROWS - 1 + ts, :]
    buf_ref[0:PAD_ROWS, :] = buf_ref[ts:ts + PAD_ROWS, :]
    p = cur + (prev - cur) * mu_ref[...]
    r = p[:, :w]
    k = p[:, w:2 * w]
    v = p[:, 2 * w:3 * w]
    lw = p[:, 3 * w:3 * w + LANES]
    xg = p[:, 3 * w + LANES:]

    sw = 2 * LANES
    seg_ones = jnp.where(_iota2((sw, sw), 0) // n == _iota2((sw, sw), 1) // n, 1.0, 0.0).astype(BF16)

    def seg_sum(t):
        return jnp.concatenate([_dot_split(t[:, j:j + sw], seg_ones) for j in range(0, w, sw)], axis=1)

    wl = w0_ref[...] + _bdot(jnp.tanh(lw), w2_ref[...])
    log_w = -jnp.exp(-_softplus(-wl) - 0.5)
    a = _sigmoid(a0_ref[...] + _bdot(lw, a2_ref[...]))
    gate = _bdot(_sigmoid(xg), g2_ref[...])
    kkr = k * kk_ref[...]
    kk = kkr * lax.rsqrt(seg_sum(kkr * kkr) + EPS)
    k_mod = k * (1.0 + (a - 1.0) * ka_ref[...])
    kka = kk * a

    cs = _split_dot(_chunk_tri(ts), log_w)
    p_inv = jnp.exp(-cs)
    rt = r * jnp.exp(cs)
    kkt = kk * jnp.exp(cs - log_w)
    kh = k_mod * p_inv
    kah = kka * p_inv

    lane = _iota2((1, LANES), 1)
    m0 = jnp.where(lane < n, 1.0, 0.0)
    m1 = 1.0 - m0
    row = _iota2((c_len, LANES), 0)
    col = _iota2((c_len, LANES), 1) % c_len
    strict2 = col < row
    incl2 = col <= row
    same_head = (_iota2((LANES, LANES), 0) // n) == (_iota2((LANES, LANES), 1) // n)
    halves = lambda t: jnp.concatenate([t * m0, t * m1], axis=0)

    n_chunks = ts // c_len
    n_pairs = RWKV_HEADS // 2
    chains = [(c, pr) for c in range(n_chunks) for pr in range(n_pairs)]
    pre = []
    for c, pr in chains:
        r0, l0 = c * c_len, pr * LANES
        sl = lambda t: t[r0:r0 + c_len, l0:l0 + LANES]
        p_last = jnp.exp(cs[r0 + c_len - 1:r0 + c_len, l0:l0 + LANES])
        pre.append((sl(kkt), sl(rt), sl(kh), sl(kah), sl(v), p_last))
    rrs = [_bdot_nt(jnp.concatenate([kkt_c, rt_c], axis=0),
                    jnp.concatenate([halves(kh_c), halves(kah_c)], axis=0))
           for kkt_c, rt_c, kh_c, kah_c, v_c, p_last in pre]
    lk2s = [jnp.where(strict2, rr[:c_len, :LANES], 0.0) for rr in rrs]
    la2s = [jnp.where(strict2, rr[:c_len, LANES:], 0.0) for rr in rrs]
    mq2s = [jnp.concatenate([jnp.where(incl2, rr[c_len:, :LANES], 0.0),
                             -jnp.where(incl2, rr[c_len:, LANES:], 0.0)], axis=1) for rr in rrs]
    vss = [halves(t[4]) for t in pre]
    lkvs = [_bdot(lk2, vs) for lk2, vs in zip(lk2s, vss)]
    tbds = _tri_inv([halves(la2) for la2 in la2s], 6)
    tws = [_bdot(tbd, jnp.concatenate([halves(t[0]), halves(lkv)], axis=1))
           for tbd, t, lkv in zip(tbds, pre, lkvs)]
    wu0 = [tw[:c_len] + tw[c_len:] for tw in tws]

    for c in range(n_chunks):
        r0 = c * c_len
        idx = [c * n_pairs + pr for pr in range(n_pairs)]
        states = [s_ref[pr] for pr in range(n_pairs)]
        xas = [_bdot_nt(jnp.concatenate([wu0[i][:, :LANES], pre[i][1]], axis=0), s)
               for i, s in zip(idx, states)]
        us = [xa[:c_len] + wu0[i][:, LANES:] for i, xa in zip(idx, xas)]
        for pr, (i, s, u) in enumerate(zip(idx, states, us)):
            kkt_c, rt_c, kh_c, kah_c, v_c, p_last = pre[i]
            upd = _bdot_tn(jnp.concatenate([v_c, u], axis=0),
                           jnp.concatenate([kh_c * p_last, -kah_c * p_last], axis=0))
            s_ref[pr] = s * p_last + jnp.where(same_head, upd, 0.0)
        for pr, (i, xa, u) in enumerate(zip(idx, xas, us)):
            l0 = pr * LANES
            y_ref[r0:r0 + c_len, l0:l0 + LANES] = xa[c_len:] + _bdot(
                mq2s[i], jnp.concatenate([vss[i], halves(u)], axis=0))

    o = y_ref[...]
    mean = seg_sum(o) * (1.0 / n)
    d = o - mean
    var = seg_sum(d * d) * (1.0 / n)
    o = d * lax.rsqrt(var + RWKV_GN_EPS) * lnw_ref[...] + lnb_ref[...]
    o = o + seg_sum(r * k_mod * rk_ref[...]) * v
    o_ref[...] = (o * gate).astype(o_ref.dtype)


def rwkv7_time_mix(p, mu, w0, w2, a0, a2, g2, k_k, k_a, r_k, ln_w, ln_b, batch, seq):
    t = p.shape[0]
    nt = seq // SEQ_TILE
    w3 = 3 * RWKV_WIDTH
    o4 = w3 + RWKV_DECAY_LORA
    o5 = o4 + RWKV_A_LORA
    row = lambda v: v.reshape(1, -1).astype(F32)
    mu_p = jnp.concatenate([mu, jnp.zeros((RWKV_XG_P - RWKV_GATE_LORA,), F32)])[None, :]
    zl = jnp.zeros((RWKV_DECAY_LORA, RWKV_WIDTH), F32)
    w2_p = jnp.concatenate([w2, zl], axis=0).astype(BF16)
    a2_p = jnp.concatenate([zl, a2], axis=0).astype(BF16)
    g2_p = jnp.concatenate([g2, jnp.zeros((RWKV_XG_P - RWKV_GATE_LORA, RWKV_WIDTH), F32)],
                           axis=0).astype(BF16)
    params = (mu_p, row(w0), w2_p, row(a0), a2_p, g2_p, row(k_k), row(k_a), row(r_k), row(ln_w), row(ln_b))
    return pl.pallas_call(
        _rwkv_kernel,
        grid=(batch, nt),
        in_specs=[pl.BlockSpec((SEQ_TILE, RWKV_P), lambda b, i: (b * nt + i, 0))]
                 + [_resident(v.shape) for v in params],
        out_specs=pl.BlockSpec((SEQ_TILE, RWKV_WIDTH), lambda b, i: (b * nt + i, 0)),
        out_shape=jax.ShapeDtypeStruct((t, RWKV_WIDTH), BF16),
        scratch_shapes=[pltpu.VMEM((PAD_ROWS + SEQ_TILE, RWKV_P), F32),
                        pltpu.VMEM((SEQ_TILE, RWKV_WIDTH), F32),
                        pltpu.VMEM((RWKV_HEADS // 2, LANES, LANES), F32)],
        compiler_params=_cparams("parallel", "arbitrary"),
        name="rwkv7_time_mix",
    )(p, *params)


def _rwkv_weights(w_in_l):
    w = w_in_l[:, OFF_RWKV:OFF_GDN]
    return jnp.concatenate([w, jnp.zeros((D_MODEL, RWKV_XG_P - RWKV_GATE_LORA), F32)], axis=1).astype(BF16)


MOE_FF_TILE = FF_EXPERT // 4


def _router_kernel(x_ref, g_ref, wr_ref, h_ref, idx_ref, wt_ref):
    h = _rms(x_ref[...], g_ref[...])
    h_ref[...] = h
    wr = wr_ref[...]
    h_hi = h.astype(BF16)
    h_lo = (h - h_hi.astype(F32)).astype(BF16)
    w_hi = wr.astype(BF16)
    w_lo = (wr - w_hi.astype(F32)).astype(BF16)
    logits = (jnp.dot(h_hi, w_hi, preferred_element_type=F32)
              + jnp.dot(h_lo, w_hi, preferred_element_type=F32)
              + jnp.dot(h_hi, w_lo, preferred_element_type=F32))
    lane = _iota2(logits.shape, 1)
    logits = jnp.where(lane < N_EXPERTS, logits, -jnp.inf)
    m1 = jnp.max(logits, axis=-1, keepdims=True)
    i1 = jnp.min(jnp.where(logits == m1, lane, LANES), axis=-1, keepdims=True)
    rest = jnp.where(lane == i1, -jnp.inf, logits)
    m2 = jnp.max(rest, axis=-1, keepdims=True)
    i2 = jnp.min(jnp.where(rest == m2, lane, LANES), axis=-1, keepdims=True)
    e = jnp.exp(m2 - m1)
    w1 = 1.0 / (1.0 + e)
    idx_ref[...] = jnp.where(lane == 0, i1, i2)
    wt_ref[...] = jnp.where(lane == 0, w1, e * w1)


def moe_route(x, g, router):
    t = x.shape[0]
    wr = jnp.concatenate([router.astype(F32), jnp.zeros((D_MODEL, LANES - N_EXPERTS), F32)], axis=1)
    return pl.pallas_call(
        _router_kernel,
        grid=(t // ROW_TILE,),
        in_specs=[_rows(ROW_TILE, D_MODEL), _resident((1, D_MODEL)), _resident(wr.shape)],
        out_specs=[_rows(ROW_TILE, D_MODEL), _rows(ROW_TILE, LANES), _rows(ROW_TILE, LANES)],
        out_shape=[jax.ShapeDtypeStruct((t, D_MODEL), F32), jax.ShapeDtypeStruct((t, LANES), jnp.int32),
                   jax.ShapeDtypeStruct((t, LANES), F32)],
        compiler_params=_cparams("parallel"),
        name="moe_router",
    )(x, g, wr)


DMA_UNROLL = 8


def _moe_plan(idx):
    t = idx.shape[0]
    n_pairs = t * TOP_K
    n_blk = -(-(n_pairs + N_EXPERTS * (MOE_BLOCK - 1)) // MOE_BLOCK)
    n_slot = n_blk * MOE_BLOCK
    flat_e = idx[:, :TOP_K].reshape(n_pairs)
    onehot = (flat_e[:, None] == jnp.arange(N_EXPERTS, dtype=jnp.int32)[None, :]).astype(jnp.int32)
    rank = jnp.sum((jnp.cumsum(onehot, axis=0) - onehot) * onehot, axis=1)
    counts = jnp.sum(onehot, axis=0)
    padded = (counts + MOE_BLOCK - 1) // MOE_BLOCK * MOE_BLOCK
    pad_end = jnp.cumsum(padded)
    pad_start = pad_end - padded
    dest = pad_start[flat_e] + rank
    pair_id = jnp.arange(n_pairs, dtype=jnp.int32)
    pair_row = (pair_id % TOP_K) * t + pair_id // TOP_K
    spare = n_pairs + jnp.arange(n_slot, dtype=jnp.int32) % MOE_BLOCK
    slot_dst = spare.at[dest].set(pair_row)
    slot_tok = jnp.where(slot_dst < n_pairs, slot_dst % t, 0)
    blk_start = jnp.arange(n_blk, dtype=jnp.int32) * MOE_BLOCK
    blk_e = jnp.minimum(jnp.sum((pad_end[None, :] <= blk_start[:, None]).astype(jnp.int32), axis=1),
                        N_EXPERTS - 1)
    return slot_tok, slot_dst, blk_e


def _moe_kernel(blk_e_ref, tok_ref, tokn_ref, dst_ref, dstp_ref, h_hbm, wg_ref, wu_ref, wd_ref, out_hbm,
                xbuf, xb_ref, acc_ref, ybuf, gsem, ssem):
    n = pl.program_id(0)
    f = pl.program_id(1)
    n_blk = pl.num_programs(0)
    n_ff = FF_EXPERT // MOE_FF_TILE
    rows_per_step = MOE_BLOCK // n_ff
    slot = n % 2

    def row_in(tok, q, j, s):
        return pltpu.make_async_copy(h_hbm.at[pl.ds(tok, 1), :], xbuf.at[s, q, pl.ds(j, 1), :], gsem.at[s])

    def row_out(q, j, d):
        return pltpu.make_async_copy(ybuf.at[q, pl.ds(j, 1), :], out_hbm.at[pl.ds(d, 1), :], ssem.at[0])

    def for_rows(fn):
        def body(r, carry):
            fn(r, r // rows_per_step, r % rows_per_step)
            return carry
        lax.fori_loop(0, MOE_BLOCK, body, 0, unroll=DMA_UNROLL)

    @pl.when(f == 0)
    def _():
        @pl.when(n == 0)
        def _():
            for_rows(lambda r, q, j: row_in(tok_ref[0, 0, r], q, j, 0).start())
            ybuf[...] = jnp.zeros(ybuf.shape, F32)

        for_rows(lambda r, q, j: row_in(tok_ref[0, 0, r], q, j, slot).wait())
        xb_ref[...] = xbuf[slot].reshape(MOE_BLOCK, D_MODEL).astype(BF16)

    r0 = f * rows_per_step
    for j in range(rows_per_step):
        row_in(tokn_ref[0, 0, r0 + j], f, j, 1 - slot).start()
        row_out(f, j, dstp_ref[0, 0, r0 + j]).start()

    xb = xb_ref[...]
    gate = jnp.dot(xb, wg_ref[0], preferred_element_type=F32)
    up = jnp.dot(xb, wu_ref[0], preferred_element_type=F32)
    part = jnp.dot((_silu(gate) * up).astype(BF16), wd_ref[0], preferred_element_type=F32)

    @pl.when(f == 0)
    def _():
        acc_ref[...] = part

    @pl.when(jnp.logical_and(f > 0, f < n_ff - 1))
    def _():
        acc_ref[...] += part

    @pl.when(f == n_ff - 1)
    def _():
        for_rows(lambda r, q, j: row_out(q, j, dstp_ref[0, 0, r]).wait())
        ybuf[...] = (acc_ref[...] + part).reshape(ybuf.shape)

        @pl.when(n == n_blk - 1)
        def _():
            for_rows(lambda r, q, j: row_out(q, j, dst_ref[0, 0, r]).start())
            for_rows(lambda r, q, j: row_out(q, j, dst_ref[0, 0, r]).wait())
            for_rows(lambda r, q, j: row_in(tokn_ref[0, 0, r], q, j, 1 - slot).wait())


def moe_experts(h, slot_tok, slot_dst, blk_e, w_gu, w_down):
    t = h.shape[0]
    n_blk = blk_e.shape[0]
    n_ff = FF_EXPERT // MOE_FF_TILE
    tok3 = slot_tok.reshape(n_blk, 1, MOE_BLOCK)
    dst3 = slot_dst.reshape(n_blk, 1, MOE_BLOCK)
    smem_blk = lambda imap: pl.BlockSpec((1, 1, MOE_BLOCK), imap, memory_space=pltpu.SMEM)
    grid_spec = pltpu.PrefetchScalarGridSpec(
        num_scalar_prefetch=1,
        grid=(n_blk, n_ff),
        in_specs=[
            smem_blk(lambda n, f, be: (n, 0, 0)),
            smem_blk(lambda n, f, be: (jnp.minimum(n + 1, be.shape[0] - 1), 0, 0)),
            smem_blk(lambda n, f, be: (n, 0, 0)),
            smem_blk(lambda n, f, be: (jnp.maximum(n - 1, 0), 0, 0)),
            pl.BlockSpec(memory_space=pl.ANY),
            pl.BlockSpec((1, D_MODEL, MOE_FF_TILE), lambda n, f, be: (be[n], 0, f)),
            pl.BlockSpec((1, D_MODEL, MOE_FF_TILE), lambda n, f, be: (be[n], 0, f + FF_EXPERT // MOE_FF_TILE)),
            pl.BlockSpec((1, MOE_FF_TILE, D_MODEL), lambda n, f, be: (be[n], f, 0)),
        ],
        out_specs=pl.BlockSpec(memory_space=pl.ANY),
        scratch_shapes=[pltpu.VMEM((2, n_ff, MOE_BLOCK // n_ff, D_MODEL), F32),
                        pltpu.VMEM((MOE_BLOCK, D_MODEL), BF16),
                        pltpu.VMEM((MOE_BLOCK, D_MODEL), F32),
                        pltpu.VMEM((n_ff, MOE_BLOCK // n_ff, D_MODEL), F32),
                        pltpu.SemaphoreType.DMA((2,)),
                        pltpu.SemaphoreType.DMA((1,))],
    )
    return pl.pallas_call(
        _moe_kernel,
        grid_spec=grid_spec,
        out_shape=jax.ShapeDtypeStruct((TOP_K * t + MOE_BLOCK, D_MODEL), F32),
        compiler_params=_cparams("arbitrary", "arbitrary"),
        name="moe_experts",
    )(blk_e, tok3, tok3, dst3, dst3, h, w_gu, w_gu, w_down)


def _moe_combine_kernel(x_ref, w_ref, y0_ref, y1_ref, *rest):
    w = w_ref[...]
    y = x_ref[...] + w[:, 0:1] * y0_ref[...] + w[:, 1:2] * y1_ref[...]
    if len(rest) == 2:
        g_ref, o_ref = rest
        o_ref[...] = _rms(y, g_ref[...])
    else:
        rest[0][...] = y


def moe_combine(x, wts, pair_out, out_norm=None):
    t = x.shape[0]
    nt = t // ROW_TILE
    extra = () if out_norm is None else (out_norm,)
    return pl.pallas_call(
        _moe_combine_kernel,
        grid=(nt,),
        in_specs=[_rows(ROW_TILE, D_MODEL), _rows(ROW_TILE, LANES), _rows(ROW_TILE, D_MODEL),
                  pl.BlockSpec((ROW_TILE, D_MODEL), lambda i: (i + nt, 0))]
                 + [_resident(v.shape) for v in extra],
        out_specs=_rows(ROW_TILE, D_MODEL),
        out_shape=jax.ShapeDtypeStruct((t, D_MODEL), F32),
        compiler_params=_cparams("parallel"),
        name="moe_combine",
    )(x, wts, pair_out, pair_out, *extra)


def moe_swiglu(x, g, router, w_gu, w_down, out_norm=None):
    h, idx, wts = moe_route(x, g, router)
    slot_tok, slot_dst, blk_e = _moe_plan(idx)
    pair_out = moe_experts(h, slot_tok, slot_dst, blk_e, w_gu, w_down)
    return moe_combine(x, wts, pair_out, out_norm)


def kernel(x, positions, norm_mix, w_in, mla_q_norm, mla_kv_norm, mla_w_uq, mla_w_ukv, rwkv_mu, rwkv_w0,
           rwkv_w2, rwkv_a0, rwkv_a2, rwkv_g2, rwkv_k_k, rwkv_k_a, rwkv_r_k, rwkv_ln_w, rwkv_ln_b, gdn_conv,
           gdn_a_log, gdn_dt_bias, gdn_norm, ssd_conv_w, ssd_conv_b, ssd_dt_bias, ssd_a_log, ssd_d, ssd_norm,
           gate_b, w_branch, w_out, norm_ffn, ffn_w_gu, ffn_w_down, moe_router, moe_w_gu, moe_w_down,
           norm_final):
    batch, seq, d = x.shape
    t = batch * seq
    depth = w_in.shape[0]
    xf = x.reshape(t, d)
    cos, sin = rope_tables(positions.reshape(t, 1).astype(jnp.int32))
    row = lambda v: v.reshape(1, -1)
    for layer in range(depth):
        w_l = w_in[layer]
        w_a, wq, wqr, wk, wv = _mla_weights(w_l, mla_w_uq[layer], mla_w_ukv[layer])
        pa, pb, pc, pd = norm_proj(xf, row(norm_mix[layer]),
                                   [w_a, _rwkv_weights(w_l), _gdn_weights(w_l), _ssd_weights(w_l)])
        y_a = mla_attention(pa, cos, sin, row(mla_q_norm[layer]), row(mla_kv_norm[layer]),
                            wq, wqr, wk, wv, batch, seq)
        y_b = rwkv7_time_mix(pb, rwkv_mu[layer], rwkv_w0[layer], rwkv_w2[layer], rwkv_a0[layer],
                             rwkv_a2[layer], rwkv_g2[layer], rwkv_k_k[layer], rwkv_k_a[layer],
                             rwkv_r_k[layer], rwkv_ln_w[layer], rwkv_ln_b[layer], batch, seq)
        y_c = gated_deltanet(pc, gdn_conv[layer], gdn_a_log[layer], gdn_dt_bias[layer], gdn_norm[layer],
                             batch, seq)
        y_d = mamba2_ssd(pd, ssd_conv_w[layer], ssd_conv_b[layer], ssd_dt_bias[layer], ssd_a_log[layer],
                         ssd_d[layer], ssd_norm[layer], batch, seq)
        w_gate = w_l[:, OFF_GATE:].reshape(D_MODEL, N_BRANCH, D_MODEL).transpose(1, 0, 2).astype(BF16)
        xf = merge(xf, row(norm_mix[layer]), w_gate, gate_b[layer], (y_a, y_b, y_c, y_d),
                   w_branch[layer].astype(BF16), w_out[layer].astype(BF16))
        last = layer == depth - 1
        if layer % 2 == 0:
            xf = ffn_dense(xf, row(norm_ffn[layer]), ffn_w_gu[layer // 2].astype(BF16),
                           ffn_w_down[layer // 2].astype(BF16))
            if last:
                xf = final_norm(xf, row(norm_final))
        else:
            xf = moe_swiglu(xf, row(norm_ffn[layer]), moe_router[layer // 2],
                            moe_w_gu[layer // 2].astype(BF16), moe_w_down[layer // 2].astype(BF16),
                            out_norm=row(norm_final) if last else None)
    return xf.reshape(batch, seq, d)
```

```python
import functools
import math

import jax
import jax.numpy as jnp
from jax import lax
from jax.experimental import pallas as pl
from jax.experimental.pallas import tpu as pltpu

F32 = jnp.float32
BF16 = jnp.bfloat16

D_MODEL = 1024
CHUNK = 64
EPS = 1e-6
MLA_HEADS = 4
MLA_Q_LORA = 384
MLA_KV_LORA = 256
MLA_NOPE = 128
MLA_ROPE = 64
MLA_V = 128
ROPE_THETA = 10000.0
RWKV_HEADS = 8
RWKV_HEAD = 64
RWKV_WIDTH = RWKV_HEADS * RWKV_HEAD
RWKV_DECAY_LORA = 64
RWKV_A_LORA = 64
RWKV_GATE_LORA = 160
RWKV_GN_EPS = 64e-5
GDN_HEADS = 4
GDN_HEAD = 128
GDN_WIDTH = GDN_HEADS * GDN_HEAD
CONV_K = 4
SSD_HEADS = 8
SSD_HEAD = 64
SSD_WIDTH = SSD_HEADS * SSD_HEAD
SSD_GROUPS = 2
SSD_STATE = 128
SSD_CONV_DIM = SSD_WIDTH + 2 * SSD_GROUPS * SSD_STATE
N_BRANCH = 4
BRANCH_WIDTH = 512
FF_DENSE = 2816
N_EXPERTS = 8
TOP_K = 2
FF_EXPERT = 3584
MOE_BLOCK = 512

MLA_COLS = MLA_Q_LORA + MLA_KV_LORA + MLA_ROPE
RWKV_COLS = 3 * RWKV_WIDTH + RWKV_DECAY_LORA + RWKV_A_LORA + RWKV_GATE_LORA
GDN_COLS = 4 * GDN_WIDTH + 2 * GDN_HEADS
SSD_COLS = SSD_WIDTH + SSD_CONV_DIM + SSD_HEADS
OFF_RWKV = MLA_COLS
OFF_GDN = OFF_RWKV + RWKV_COLS
OFF_SSD = OFF_GDN + GDN_COLS
OFF_GATE = OFF_SSD + SSD_COLS

LANES = 128
SUBLANES = 8
VMEM_LIMIT_BYTES = 56 * 2**20

ROW_TILE = 256
SEQ_TILE = 256
ATT_TILE = 256
PAD_ROWS = SUBLANES

MLA_P = MLA_Q_LORA + MLA_KV_LORA + 2 * LANES
RWKV_P = 3 * RWKV_WIDTH + LANES + 2 * LANES
GDN_P = 4 * GDN_WIDTH + LANES
SSD_P = SSD_WIDTH + SSD_CONV_DIM + LANES


def _cparams(*sem):
    return pltpu.CompilerParams(dimension_semantics=sem, vmem_limit_bytes=VMEM_LIMIT_BYTES)


def _resident(shape):
    nd = len(shape)
    return pl.BlockSpec(shape, lambda *_: (0,) * nd, pipeline_mode=pl.Buffered(1))


def _rows(tile, width):
    return pl.BlockSpec((tile, width), lambda i: (i, 0))


def _bdot(a, b):
    return jnp.dot(a.astype(BF16), b.astype(BF16), preferred_element_type=F32)


def _bdot_nt(a, b):
    return lax.dot_general(a.astype(BF16), b.astype(BF16), (((1,), (1,)), ((), ())),
                           preferred_element_type=F32)


def _bdot_tn(a, b):
    return lax.dot_general(a.astype(BF16), b.astype(BF16), (((0,), (0,)), ((), ())),
                           preferred_element_type=F32)


def _split_dot(m01, x):
    hi = x.astype(BF16)
    lo = (x - hi.astype(F32)).astype(BF16)
    return (jnp.dot(m01, hi, preferred_element_type=F32)
            + jnp.dot(m01, lo, preferred_element_type=F32))


def _dot_split(x, m01):
    hi = x.astype(BF16)
    lo = (x - hi.astype(F32)).astype(BF16)
    return (jnp.dot(hi, m01, preferred_element_type=F32)
            + jnp.dot(lo, m01, preferred_element_type=F32))


def _rms(x, g):
    return x * lax.rsqrt(jnp.mean(x * x, axis=-1, keepdims=True) + EPS) * g


def _sigmoid(x):
    return 1.0 / (1.0 + jnp.exp(-x))


def _silu(x):
    return x * _sigmoid(x)


def _softplus(x):
    return jnp.maximum(x, 0.0) + jnp.log(1.0 + jnp.exp(-jnp.abs(x)))


def _iota2(shape, dim):
    return lax.broadcasted_iota(jnp.int32, shape, dim)


def _tri_inv(lows, steps):
    n = lows[0].shape[0]
    eye = (_iota2((n, n), 0) == _iota2((n, n), 1)).astype(F32)
    ps = [eye - low for low in lows]
    qs = [_bdot(low, low) for low in lows]
    for _ in range(steps - 2):
        rs = [_bdot(jnp.concatenate([p, q], axis=0), q) for p, q in zip(ps, qs)]
        ps = [p + r[:n] for p, r in zip(ps, rs)]
        qs = [r[n:] for r in rs]
    return [p + _bdot(p, q) for p, q in zip(ps, qs)]


MXU_COLS = 2 * LANES


def _norm_proj_kernel(x_ref, g_ref, w1_ref, w2_ref, oa_ref, ob_ref, oc_ref, od_ref):
    h = _rms(x_ref[...], g_ref[...]).astype(BF16)
    for w_ref, (p_ref, q_ref) in ((w1_ref, (oa_ref, od_ref)), (w2_ref, (ob_ref, oc_ref))):
        res = jnp.dot(h, w_ref[...], preferred_element_type=F32)
        n = p_ref.shape[1]
        p_ref[...] = res[:, :n]
        q_ref[...] = res[:, n:]


def norm_proj(x, g, weights):
    t = x.shape[0]
    wa, wb, wc, wd = weights
    widths = [w.shape[1] for w in weights]
    w1 = jnp.concatenate([wa, wd], axis=1)
    w2 = jnp.concatenate([wb, wc], axis=1)
    assert w1.shape[1] % MXU_COLS == 0 and w2.shape[1] % MXU_COLS == 0
    return pl.pallas_call(
        _norm_proj_kernel,
        grid=(t // ROW_TILE,),
        in_specs=[_rows(ROW_TILE, D_MODEL), _resident((1, D_MODEL)), _resident(w1.shape), _resident(w2.shape)],
        out_specs=[_rows(ROW_TILE, n) for n in widths],
        out_shape=[jax.ShapeDtypeStruct((t, n), F32) for n in widths],
        compiler_params=_cparams("parallel"),
        name="norm_proj",
    )(x, g, w1, w2)


def _rope_table_kernel(pos_ref, freq_ref, cos_ref, sin_ref):
    ang = pos_ref[...].astype(F32) * freq_ref[...]
    cos_ref[...] = jnp.cos(ang)
    sin_ref[...] = jnp.sin(ang)


def rope_tables(pos_col):
    t = pos_col.shape[0]
    half = MLA_ROPE // 2
    inv_freq = ROPE_THETA ** (-jnp.arange(half, dtype=F32) / half)
    freq = jnp.concatenate([inv_freq, inv_freq, jnp.zeros((LANES - MLA_ROPE,), F32)])[None, :]
    tile = min(1024, t)
    return pl.pallas_call(
        _rope_table_kernel,
        grid=(t // tile,),
        in_specs=[_rows(tile, 1), _resident((1, LANES))],
        out_specs=[_rows(tile, LANES), _rows(tile, LANES)],
        out_shape=[jax.ShapeDtypeStruct((t, LANES), F32)] * 2,
        compiler_params=_cparams("parallel"),
        name="rope_tables",
    )(pos_col, freq)


def _mla_prep_kernel(p_ref, cos_ref, sin_ref, qn_ref, kvn_ref, wq_ref, wqr_ref, wk_ref, wv_ref,
                     q_ref, k_ref, v_ref):
    cos = cos_ref[...]
    sin = sin_ref[...]
    scale = (MLA_NOPE + MLA_ROPE) ** -0.5
    nq = _rms(p_ref[:, :MLA_Q_LORA], qn_ref[...]).astype(BF16)
    q = jnp.dot(nq, wq_ref[...], preferred_element_type=F32)
    qr = jnp.dot(nq, wqr_ref[...], preferred_element_type=F32)
    nkv = _rms(p_ref[:, MLA_Q_LORA:MLA_Q_LORA + MLA_KV_LORA], kvn_ref[...]).astype(BF16)
    kn = jnp.dot(nkv, wk_ref[...], preferred_element_type=F32)
    v_ref[...] = jnp.dot(nkv, wv_ref[...], preferred_element_type=F32).astype(BF16)
    o0 = MLA_Q_LORA + MLA_KV_LORA
    k_rope = (p_ref[:, o0:o0 + LANES] * cos + p_ref[:, o0 + LANES:o0 + 2 * LANES] * sin).astype(BF16)
    for h in range(MLA_HEADS):
        c0 = 2 * LANES * h
        q_ref[:, c0:c0 + LANES] = (q[:, c0:c0 + LANES] * scale).astype(BF16)
        q_rope = q[:, c0 + LANES:c0 + 2 * LANES] * cos + qr[:, h * LANES:(h + 1) * LANES] * sin
        q_ref[:, c0 + LANES:c0 + 2 * LANES] = (q_rope * scale).astype(BF16)
        k_ref[:, c0:c0 + LANES] = kn[:, h * LANES:(h + 1) * LANES].astype(BF16)
        k_ref[:, c0 + LANES:c0 + 2 * LANES] = k_rope


def _mla_attn_kernel(q_ref, k_ref, v_ref, o_ref):
    i = pl.program_id(1)
    tq = ATT_TILE
    hq = 2 * LANES
    heads = range(MLA_HEADS)
    qs = [q_ref[:, h * hq:(h + 1) * hq] for h in heads]

    def tile(j, carry, allowed):
        rows = pl.ds(pl.multiple_of(j * ATT_TILE, ATT_TILE), ATT_TILE)
        ss = [lax.dot_general(qs[h], k_ref[rows, h * hq:(h + 1) * hq], (((1,), (1,)), ((), ())),
                              preferred_element_type=F32) for h in heads]
        if allowed is not None:
            ss = [jnp.where(allowed, s, -jnp.inf) for s in ss]
        m_new = [jnp.maximum(carry[h][0], jnp.max(ss[h], axis=-1, keepdims=True)) for h in heads]
        alpha = [jnp.exp(carry[h][0] - m_new[h]) for h in heads]
        ps = [jnp.exp(ss[h] - m_new[h]) for h in heads]
        pv = [jnp.dot(ps[h].astype(BF16), v_ref[rows, h * MLA_V:(h + 1) * MLA_V],
                      preferred_element_type=F32) for h in heads]
        return tuple((m_new[h], carry[h][1] * alpha[h] + jnp.sum(ps[h], axis=-1, keepdims=True),
                      carry[h][2] * alpha[h] + pv[h]) for h in heads)

    init = tuple((jnp.full((tq, 1), -jnp.inf, F32), jnp.zeros((tq, 1), F32), jnp.zeros((tq, MLA_V), F32))
                 for _ in heads)
    carry = lax.fori_loop(0, i, lambda j, c: tile(j, c, None), init)
    allowed = (_iota2((tq, ATT_TILE), 1) // CHUNK) <= (_iota2((tq, ATT_TILE), 0) // CHUNK)
    final = tile(i, carry, allowed)
    for h in heads:
        o_ref[:, h * MLA_V:(h + 1) * MLA_V] = (final[h][2] / final[h][1]).astype(o_ref.dtype)


def mla_attention(p, cos, sin, q_norm, kv_norm, wq, wqr, wk, wv, batch, seq):
    t = p.shape[0]
    hq = 2 * LANES
    q, k, v = pl.pallas_call(
        _mla_prep_kernel,
        grid=(t // ROW_TILE,),
        in_specs=[_rows(ROW_TILE, MLA_P), _rows(ROW_TILE, LANES), _rows(ROW_TILE, LANES),
                  _resident(q_norm.shape), _resident(kv_norm.shape), _resident(wq.shape),
                  _resident(wqr.shape), _resident(wk.shape), _resident(wv.shape)],
        out_specs=[_rows(ROW_TILE, MLA_HEADS * hq), _rows(ROW_TILE, MLA_HEADS * hq),
                   _rows(ROW_TILE, MLA_HEADS * MLA_V)],
        out_shape=[jax.ShapeDtypeStruct((t, MLA_HEADS * hq), BF16),
                   jax.ShapeDtypeStruct((t, MLA_HEADS * hq), BF16),
                   jax.ShapeDtypeStruct((t, MLA_HEADS * MLA_V), BF16)],
        compiler_params=_cparams("parallel"),
        name="mla_prep",
    )(p, cos, sin, q_norm, kv_norm, wq, wqr, wk, wv)
    nq = seq // ATT_TILE
    return pl.pallas_call(
        _mla_attn_kernel,
        grid=(batch, nq),
        in_specs=[pl.BlockSpec((ATT_TILE, MLA_HEADS * hq), lambda b, i: (b * nq + i, 0)),
                  pl.BlockSpec((seq, MLA_HEADS * hq), lambda b, i: (b, 0)),
                  pl.BlockSpec((seq, MLA_HEADS * MLA_V), lambda b, i: (b, 0))],
        out_specs=pl.BlockSpec((ATT_TILE, MLA_HEADS * MLA_V), lambda b, i: (b * nq + i, 0)),
        out_shape=jax.ShapeDtypeStruct((t, MLA_HEADS * MLA_V), BF16),
        compiler_params=_cparams("parallel", "arbitrary"),
        name="mla_attn",
    )(q, k, v)


def _mla_weights(w_in_l, w_uq, w_ukv):
    half = MLA_ROPE // 2
    zpad = lambda rows, n: jnp.zeros((rows, n), F32)
    o0 = MLA_Q_LORA + MLA_KV_LORA
    kr = w_in_l[:, o0:o0 + MLA_ROPE]
    kr_rot = jnp.concatenate([-kr[:, half:], kr[:, :half]], axis=1)
    w_a = jnp.concatenate([w_in_l[:, :o0], kr, zpad(D_MODEL, LANES - MLA_ROPE),
                           kr_rot, zpad(D_MODEL, LANES - MLA_ROPE)], axis=1)
    uq = w_uq.reshape(MLA_Q_LORA, MLA_HEADS, MLA_NOPE + MLA_ROPE)
    rope_w = uq[:, :, MLA_NOPE:]
    wq = jnp.concatenate([uq, jnp.zeros((MLA_Q_LORA, MLA_HEADS, LANES - MLA_ROPE), F32)], axis=2)
    wqr = jnp.concatenate([-rope_w[:, :, half:], rope_w[:, :, :half],
                           jnp.zeros((MLA_Q_LORA, MLA_HEADS, LANES - MLA_ROPE), F32)], axis=2)
    ukv = w_ukv.reshape(MLA_KV_LORA, MLA_HEADS, MLA_NOPE + MLA_V)
    wk = ukv[:, :, :MLA_NOPE].reshape(MLA_KV_LORA, MLA_HEADS * MLA_NOPE)
    wv = ukv[:, :, MLA_NOPE:].reshape(MLA_KV_LORA, MLA_HEADS * MLA_V)
    return (w_a.astype(BF16), wq.reshape(MLA_Q_LORA, -1).astype(BF16),
            wqr.reshape(MLA_Q_LORA, -1).astype(BF16), wk.astype(BF16), wv.astype(BF16))


def _merge_kernel(x_ref, g_ref, wg_ref, gb_ref, ya_ref, yb_ref, yc_ref, yd_ref, wb_ref, wo_ref, o_ref):
    x = x_ref[...]
    h = _rms(x, g_ref[...]).astype(BF16)
    merged = None
    for i, y_ref in enumerate((ya_ref, yb_ref, yc_ref, yd_ref)):
        gate = _sigmoid(jnp.dot(h, wg_ref[i], preferred_element_type=F32) + gb_ref[i:i + 1, :])
        term = gate * jnp.dot(y_ref[...], wb_ref[i], preferred_element_type=F32)
        merged = term if merged is None else merged + term
    o_ref[...] = x + jnp.dot(merged.astype(BF16), wo_ref[...], preferred_element_type=F32)


def merge(x, g, w_gate, gate_b, ys, w_branch, w_out):
    t = x.shape[0]
    return pl.pallas_call(
        _merge_kernel,
        grid=(t // ROW_TILE,),
        in_specs=[_rows(ROW_TILE, D_MODEL), _resident((1, D_MODEL)), _resident(w_gate.shape),
                  _resident(gate_b.shape)] + [_rows(ROW_TILE, BRANCH_WIDTH)] * N_BRANCH
                 + [_resident(w_branch.shape), _resident(w_out.shape)],
        out_specs=_rows(ROW_TILE, D_MODEL),
        out_shape=jax.ShapeDtypeStruct((t, D_MODEL), F32),
        compiler_params=_cparams("parallel"),
        name="merge",
    )(x, g, w_gate, gate_b, *ys, w_branch, w_out)


FF_SPLIT = (FF_DENSE // MXU_COLS // 2) * MXU_COLS


def _ffn_kernel(x_ref, g_ref, wgu_ref, wd_ref, o_ref):
    x = x_ref[...]
    h = _rms(x, g_ref[...]).astype(BF16)
    acc = x
    for c0, c1 in ((0, FF_SPLIT), (FF_SPLIT, FF_DENSE)):
        gate = jnp.dot(h, wgu_ref[:, c0:c1], preferred_element_type=F32)
        up = jnp.dot(h, wgu_ref[:, FF_DENSE + c0:FF_DENSE + c1], preferred_element_type=F32)
        act = (_silu(gate) * up).astype(BF16)
        acc = acc + jnp.dot(act, wd_ref[c0:c1, :], preferred_element_type=F32)
    o_ref[...] = acc


def ffn_dense(x, g, w_gu, w_down):
    t = x.shape[0]
    return pl.pallas_call(
        _ffn_kernel,
        grid=(t // ROW_TILE,),
        in_specs=[_rows(ROW_TILE, D_MODEL), _resident((1, D_MODEL)), _resident(w_gu.shape),
                  _resident(w_down.shape)],
        out_specs=_rows(ROW_TILE, D_MODEL),
        out_shape=jax.ShapeDtypeStruct((t, D_MODEL), F32),
        compiler_params=_cparams("parallel"),
        name="ffn_dense",
    )(x, g, w_gu, w_down)


def _final_norm_kernel(x_ref, g_ref, o_ref):
    o_ref[...] = _rms(x_ref[...], g_ref[...])


def final_norm(x, g):
    t = x.shape[0]
    tile = min(1024, t)
    return pl.pallas_call(
        _final_norm_kernel,
        grid=(t // tile,),
        in_specs=[_rows(tile, D_MODEL), _resident((1, D_MODEL))],
        out_specs=_rows(tile, D_MODEL),
        out_shape=jax.ShapeDtypeStruct((t, D_MODEL), F32),
        compiler_params=_cparams("parallel"),
        name="final_norm",
    )(x, g)


def _causal_conv(buf_ref, x, w_ref):
    tile = x.shape[0]
    buf_ref[PAD_ROWS:PAD_ROWS + tile, :] = x
    acc = None
    for j in range(CONV_K):
        off = PAD_ROWS - (CONV_K - 1) + j
        term = buf_ref[off:off + tile, :] * w_ref[j:j + 1, :]
        acc = term if acc is None else acc + term
    buf_ref[0:PAD_ROWS, :] = buf_ref[tile:tile + PAD_ROWS, :]
    return acc


def _chunk_tri(tile):
    r = _iota2((tile, tile), 0)
    c = _iota2((tile, tile), 1)
    return jnp.where(c <= r, jnp.where(r // CHUNK == c // CHUNK, 1.0, 0.0), 0.0).astype(BF16)


def _head_expand(width):
    n = LANES * width
    return jnp.where(_iota2((LANES, n), 1) // width == _iota2((LANES, n), 0), 1.0, 0.0).astype(BF16)


def _ssd_kernel(p_ref, cw_ref, cb_ref, dtb_ref, a_ref, d_ref, nw_ref, o_ref, buf_ref, y_ref, h_ref):
    ts = SEQ_TILE
    c_len = CHUNK
    w = SSD_WIDTH
    gn = SSD_GROUPS * SSD_STATE

    @pl.when(pl.program_id(1) == 0)
    def _():
        buf_ref[0:PAD_ROWS, :] = jnp.zeros((PAD_ROWS, SSD_CONV_DIM), F32)
        h_ref[...] = jnp.zeros(h_ref.shape, F32)

    z = p_ref[:, :w]
    xbc = _silu(_causal_conv(buf_ref, p_ref[:, w:w + SSD_CONV_DIM], cw_ref) + cb_ref[...])
    x = xbc[:, :w]
    bm = xbc[:, w:w + gn]
    cm = xbc[:, w + gn:]
    dt = _softplus(p_ref[:, w + SSD_CONV_DIM:] + dtb_ref[...])
    a = dt * a_ref[...]
    acum = _split_dot(_chunk_tri(ts), a)
    expand = _head_expand(SSD_HEAD)[:, :w]
    dt_x = _dot_split(dt, expand)
    acum_x = _dot_split(acum, expand)
    xdt = x * dt_x
    acum_next = pltpu.roll(acum, LANES - 1, axis=1)
    lane = _iota2((c_len, LANES), 1)
    left = lane < SSD_HEAD
    causal2 = (lane % c_len) <= _iota2((c_len, LANES), 0)

    for c in range(ts // c_len):
        r0 = c * c_len
        ac = acum[r0:r0 + c_len]
        at = jnp.concatenate([ac, acum_next[r0:r0 + c_len]], axis=0).T
        ax = acum_x[r0:r0 + c_len]
        a_last = ax[c_len - 1:c_len, :]
        e_in = jnp.exp(ax)
        xdt_c = xdt[r0:r0 + c_len]
        xdt_d = xdt_c * jnp.exp(a_last - ax)
        e_last = jnp.exp(a_last)
        for g in range(SSD_GROUPS):
            bm_g = bm[r0:r0 + c_len, g * SSD_STATE:(g + 1) * SSD_STATE]
            cm_g = cm[r0:r0 + c_len, g * SSD_STATE:(g + 1) * SSD_STATE]
            cb2 = _bdot_nt(cm_g, jnp.concatenate([bm_g, bm_g], axis=0))
            for pp in range(SSD_HEADS // SSD_GROUPS // 2):
                p = g * (SSD_HEADS // SSD_GROUPS // 2) + pp
                l0 = p * LANES
                col2 = jnp.where(left, ac[:, 2 * p:2 * p + 1], ac[:, 2 * p + 1:2 * p + 2])
                row2 = at[2 * p:2 * p + 1, :]
                seg = jnp.exp(jnp.where(causal2, col2 - row2, -jnp.inf))
                xp = xdt_c[:, l0:l0 + LANES]
                xs = jnp.concatenate([jnp.where(left, xp, 0.0), jnp.where(left, 0.0, xp)], axis=0)
                y_diag = _bdot(cb2 * seg, xs)
                hp = h_ref[p]
                y_off = _bdot(cm_g, hp) * e_in[:, l0:l0 + LANES]
                h_ref[p] = hp * e_last[:, l0:l0 + LANES] + _bdot_tn(bm_g, xdt_d[:, l0:l0 + LANES])
                y_ref[r0:r0 + c_len, l0:l0 + LANES] = (
                    y_diag + y_off + d_ref[:, l0:l0 + LANES] * x[r0:r0 + c_len, l0:l0 + LANES])

    y = y_ref[...] * _silu(z)
    gw = w // SSD_GROUPS
    for g in range(SSD_GROUPS):
        yg = y[:, g * gw:(g + 1) * gw]
        o_ref[:, g * gw:(g + 1) * gw] = _rms(yg, nw_ref[:, g * gw:(g + 1) * gw]).astype(o_ref.dtype)


def mamba2_ssd(p, conv_w, conv_b, dt_bias, a_log, d_skip, norm_w, batch, seq):
    t = p.shape[0]
    nt = seq // SEQ_TILE
    pad = lambda v: jnp.concatenate([v.astype(F32), jnp.zeros((LANES - v.shape[0],), F32)])[None, :]
    a_row = pad(-jnp.exp(a_log.astype(F32)))
    d_row = jnp.repeat(d_skip.astype(F32), SSD_HEAD)[None, :]
    params = (conv_w, conv_b[None, :], pad(dt_bias), a_row, d_row, norm_w[None, :])
    return pl.pallas_call(
        _ssd_kernel,
        grid=(batch, nt),
        in_specs=[pl.BlockSpec((SEQ_TILE, SSD_P), lambda b, i: (b * nt + i, 0))]
                 + [_resident(v.shape) for v in params],
        out_specs=pl.BlockSpec((SEQ_TILE, SSD_WIDTH), lambda b, i: (b * nt + i, 0)),
        out_shape=jax.ShapeDtypeStruct((t, SSD_WIDTH), BF16),
        scratch_shapes=[pltpu.VMEM((PAD_ROWS + SEQ_TILE, SSD_CONV_DIM), F32),
                        pltpu.VMEM((SEQ_TILE, SSD_WIDTH), F32),
                        pltpu.VMEM((SSD_HEADS // 2, SSD_STATE, LANES), F32)],
        compiler_params=_cparams("parallel", "arbitrary"),
        name="mamba2_ssd",
    )(p, *params)


def _ssd_weights(w_in_l):
    w = w_in_l[:, OFF_SSD:OFF_GATE]
    return jnp.concatenate([w, jnp.zeros((D_MODEL, LANES - SSD_HEADS), F32)], axis=1).astype(BF16)


def _l2n(x):
    return x * lax.rsqrt(jnp.sum(x * x, axis=-1, keepdims=True) + EPS)


def _gdn_kernel(p_ref, cw_ref, bias_ref, arow_ref, nw_ref, o_ref, buf_ref, s_ref):
    ts = SEQ_TILE
    c_len = CHUNK
    w = GDN_WIDTH
    dh = GDN_HEAD

    @pl.when(pl.program_id(1) == 0)
    def _():
        buf_ref[0:PAD_ROWS, :] = jnp.zeros((PAD_ROWS, 3 * w), F32)
        s_ref[...] = jnp.zeros(s_ref.shape, F32)

    qkv = _silu(_causal_conv(buf_ref, p_ref[:, :3 * w], cw_ref))
    ba = p_ref[:, 4 * w:]
    beta = _sigmoid(ba)
    g = arow_ref[...] * _softplus(ba + bias_ref[...])
    gcum = _split_dot(_chunk_tri(ts), g)
    rr = _iota2((c_len, c_len), 0)
    cc = _iota2((c_len, c_len), 1)
    incl = cc <= rr
    strict = cc < rr

    n_chunks = ts // c_len
    chains = [(c, h) for c in range(n_chunks) for h in range(GDN_HEADS)]
    gts = []
    for c in range(n_chunks):
        gc_all = gcum[c * c_len:(c + 1) * c_len]
        gts.append(jnp.concatenate([gc_all, gc_all], axis=0).T)
    pre = []
    for c, h in chains:
        r0, l0 = c * c_len, h * dh
        q = _l2n(qkv[r0:r0 + c_len, l0:l0 + dh]) * dh ** -0.5
        k = _l2n(qkv[r0:r0 + c_len, w + l0:w + l0 + dh])
        v = qkv[r0:r0 + c_len, 2 * w + l0:2 * w + l0 + dh]
        b = beta[r0:r0 + c_len, h:h + 1]
        gcol = gcum[r0:r0 + c_len, GDN_HEADS + h:GDN_HEADS + h + 1]
        grow = gts[c][GDN_HEADS + h:GDN_HEADS + h + 1, :c_len]
        decay = jnp.exp(jnp.where(incl, gcol - grow, -jnp.inf))
        pre.append((q, k, v, b, gcol, decay, k * b))
    rs = [_bdot_nt(jnp.concatenate([kb, q], axis=0), k) for q, k, v, b, gcol, decay, kb in pre]
    lows = [jnp.where(strict, r[:c_len] * t[5], 0.0) for r, t in zip(rs, pre)]
    qks = [r[c_len:] * t[5] for r, t in zip(rs, pre)]
    tms = _tri_inv(lows, 6)
    egs = [jnp.exp(t[4]) for t in pre]
    uws = [_bdot(tm, jnp.concatenate([v * b, kb * eg], axis=1))
           for tm, eg, (q, k, v, b, gcol, decay, kb) in zip(tms, egs, pre)]

    for c in range(n_chunks):
        r0 = c * c_len
        idx = [c * GDN_HEADS + h for h in range(GDN_HEADS)]
        states = [s_ref[h] for h in range(GDN_HEADS)]
        wss = [_bdot(jnp.concatenate([uws[i][:, dh:], pre[i][0] * egs[i]], axis=0), s)
               for i, s in zip(idx, states)]
        v_news = [uws[i][:, :dh] - ws[:c_len] for i, ws in zip(idx, wss)]
        for h, (i, s, v_new) in enumerate(zip(idx, states, v_news)):
            k, gcol = pre[i][1], pre[i][4]
            g_last = gcol[c_len - 1:c_len, :]
            s_ref[h] = s * jnp.exp(g_last) + _bdot_tn(k * jnp.exp(g_last - gcol), v_new)
        for h, (i, ws, v_new) in enumerate(zip(idx, wss, v_news)):
            l0 = h * dh
            o = ws[c_len:] + _bdot(qks[i], v_new)
            zc = p_ref[r0:r0 + c_len, 3 * w + l0:3 * w + l0 + dh]
            o_ref[r0:r0 + c_len, l0:l0 + dh] = (_rms(o, nw_ref[...]) * _silu(zc)).astype(o_ref.dtype)


def gated_deltanet(p, conv_w, a_log, dt_bias, norm_w, batch, seq):
    t = p.shape[0]
    nt = seq // SEQ_TILE
    zeros = jnp.zeros((GDN_HEADS,), F32)
    tail = jnp.zeros((LANES - 2 * GDN_HEADS,), F32)
    bias = jnp.concatenate([zeros, dt_bias.astype(F32), tail])[None, :]
    a_row = jnp.concatenate([zeros, -jnp.exp(a_log.astype(F32)), tail])[None, :]
    params = (conv_w, bias, a_row, norm_w[None, :])
    return pl.pallas_call(
        _gdn_kernel,
        grid=(batch, nt),
        in_specs=[pl.BlockSpec((SEQ_TILE, GDN_P), lambda b, i: (b * nt + i, 0))]
                 + [_resident(v.shape) for v in params],
        out_specs=pl.BlockSpec((SEQ_TILE, GDN_WIDTH), lambda b, i: (b * nt + i, 0)),
        out_shape=jax.ShapeDtypeStruct((t, GDN_WIDTH), BF16),
        scratch_shapes=[pltpu.VMEM((PAD_ROWS + SEQ_TILE, 3 * GDN_WIDTH), F32),
                        pltpu.VMEM((GDN_HEADS, GDN_HEAD, GDN_HEAD), F32)],
        compiler_params=_cparams("parallel", "arbitrary"),
        name="gated_deltanet",
    )(p, *params)


def _gdn_weights(w_in_l):
    w = w_in_l[:, OFF_GDN:OFF_SSD]
    qkv = w[:, :3 * GDN_WIDTH]
    ba = w[:, 3 * GDN_WIDTH:3 * GDN_WIDTH + 2 * GDN_HEADS]
    z = w[:, 3 * GDN_WIDTH + 2 * GDN_HEADS:]
    return jnp.concatenate([qkv, z, ba, jnp.zeros((D_MODEL, LANES - 2 * GDN_HEADS), F32)],
                           axis=1).astype(BF16)


RWKV_XG_P = 2 * LANES


def _rwkv_kernel(p_ref, mu_ref, w0_ref, w2_ref, a0_ref, a2_ref, g2_ref, kk_ref, ka_ref, rk_ref,
                 lnw_ref, lnb_ref, o_ref, buf_ref, y_ref, s_ref):
    ts = SEQ_TILE
    c_len = CHUNK
    w = RWKV_WIDTH
    n = RWKV_HEAD

    @pl.when(pl.program_id(1) == 0)
    def _():
        buf_ref[0:PAD_ROWS, :] = jnp.zeros((PAD_ROWS, RWKV_P), F32)
        s_ref[...] = jnp.zeros(s_ref.shape, F32)

    cur = p_ref[...]
    buf_ref[PAD_ROWS:PAD_ROWS + ts, :] = cur
    prev = buf_ref[PAD_ROWS - 1:PAD_ROWS - 1 + ts, :]
    buf_ref[0:PAD_ROWS, :] = buf_ref[ts:ts + PAD_ROWS, :]
    p = cur + (prev - cur) * mu_ref[...]
    r = p[:, :w]
    k = p[:, w:2 * w]
    v = p[:, 2 * w:3 * w]
    lw = p[:, 3 * w:3 * w + LANES]
    xg = p[:, 3 * w + LANES:]

    sw = 2 * LANES
    seg_ones = jnp.where(_iota2((sw, sw), 0) // n == _iota2((sw, sw), 1) // n, 1.0, 0.0).astype(BF16)

    def seg_sum(t):
        return jnp.concatenate([_dot_split(t[:, j:j + sw], seg_ones) for j in range(0, w, sw)], axis=1)

    wl = w0_ref[...] + _bdot(jnp.tanh(lw), w2_ref[...])
    log_w = -jnp.exp(-_softplus(-wl) - 0.5)
    a = _sigmoid(a0_ref[...] + _bdot(lw, a2_ref[...]))
    gate = _bdot(_sigmoid(xg), g2_ref[...])
    kkr = k * kk_ref[...]
    kk = kkr * lax.rsqrt(seg_sum(kkr * kkr) + EPS)
    k_mod = k * (1.0 + (a - 1.0) * ka_ref[...])
    kka = kk * a

    cs = _split_dot(_chunk_tri(ts), log_w)
    p_inv = jnp.exp(-cs)
    rt = r * jnp.exp(cs)
    kkt = kk * jnp.exp(cs - log_w)
    kh = k_mod * p_inv
    kah = kka * p_inv

    lane = _iota2((1, LANES), 1)
    m0 = jnp.where(lane < n, 1.0, 0.0)
    m1 = 1.0 - m0
    row = _iota2((c_len, LANES), 0)
    col = _iota2((c_len, LANES), 1) % c_len
    strict2 = col < row
    incl2 = col <= row
    same_head = (_iota2((LANES, LANES), 0) // n) == (_iota2((LANES, LANES), 1) // n)
    halves = lambda t: jnp.concatenate([t * m0, t * m1], axis=0)

    n_chunks = ts // c_len
    n_pairs = RWKV_HEADS // 2
    chains = [(c, pr) for c in range(n_chunks) for pr in range(n_pairs)]
    pre = []
    for c, pr in chains:
        r0, l0 = c * c_len, pr * LANES
        sl = lambda t: t[r0:r0 + c_len, l0:l0 + LANES]
        p_last = jnp.exp(cs[r0 + c_len - 1:r0 + c_len, l0:l0 + LANES])
        pre.append((sl(kkt), sl(rt), sl(kh), sl(kah), sl(v), p_last))
    rrs = [_bdot_nt(jnp.concatenate([kkt_c, rt_c], axis=0),
                    jnp.concatenate([halves(kh_c), halves(kah_c)], axis=0))
           for kkt_c, rt_c, kh_c, kah_c, v_c, p_last in pre]
    lk2s = [jnp.where(strict2, rr[:c_len, :LANES], 0.0) for rr in rrs]
    la2s = [jnp.where(strict2, rr[:c_len, LANES:], 0.0) for rr in rrs]
    mq2s = [jnp.concatenate([jnp.where(incl2, rr[c_len:, :LANES], 0.0),
                             -jnp.where(incl2, rr[c_len:, LANES:], 0.0)], axis=1) for rr in rrs]
    vss = [halves(t[4]) for t in pre]
    lkvs = [_bdot(lk2, vs) for lk2, vs in zip(lk2s, vss)]
    tbds = _tri_inv([halves(la2) for la2 in la2s], 6)
    tws = [_bdot(tbd, jnp.concatenate([halves(t[0]), halves(lkv)], axis=1))
           for tbd, t, lkv in zip(tbds, pre, lkvs)]
    wu0 = [tw[:c_len] + tw[c_len:] for tw in tws]

    for c in range(n_chunks):
        r0 = c * c_len
        idx = [c * n_pairs + pr for pr in range(n_pairs)]
        states = [s_ref[pr] for pr in range(n_pairs)]
        xas = [_bdot_nt(jnp.concatenate([wu0[i][:, :LANES], pre[i][1]], axis=0), s)
               for i, s in zip(idx, states)]
        us = [xa[:c_len] + wu0[i][:, LANES:] for i, xa in zip(idx, xas)]
        for pr, (i, s, u) in enumerate(zip(idx, states, us)):
            kkt_c, rt_c, kh_c, kah_c, v_c, p_last = pre[i]
            upd = _bdot_tn(jnp.concatenate([v_c, u], axis=0),
                           jnp.concatenate([kh_c * p_last, -kah_c * p_last], axis=0))
            s_ref[pr] = s * p_last + jnp.where(same_head, upd, 0.0)
        for pr, (i, xa, u) in enumerate(zip(idx, xas, us)):
            l0 = pr * LANES
            y_ref[r0:r0 + c_len, l0:l0 + LANES] = xa[c_len:] + _bdot(
                mq2s[i], jnp.concatenate([vss[i], halves(u)], axis=0))

    o = y_ref[...]
    mean = seg_sum(o) * (1.0 / n)
    d = o - mean
    var = seg_sum(d * d) * (1.0 / n)
    o = d * lax.rsqrt(var + RWKV_GN_EPS) * lnw_ref[...] + lnb_ref[...]
    o = o + seg_sum(r * k_mod * rk_ref[...]) * v
    o_ref[...] = (o * gate).astype(o_ref.dtype)


def rwkv7_time_mix(p, mu, w0, w2, a0, a2, g2, k_k, k_a, r_k, ln_w, ln_b, batch, seq):
    t = p.shape[0]
    nt = seq // SEQ_TILE
    w3 = 3 * RWKV_WIDTH
    o4 = w3 + RWKV_DECAY_LORA
    o5 = o4 + RWKV_A_LORA
    row = lambda v: v.reshape(1, -1).astype(F32)
    mu_p = jnp.concatenate([mu, jnp.zeros((RWKV_XG_P - RWKV_GATE_LORA,), F32)])[None, :]
    zl = jnp.zeros((RWKV_DECAY_LORA, RWKV_WIDTH), F32)
    w2_p = jnp.concatenate([w2, zl], axis=0).astype(BF16)
    a2_p = jnp.concatenate([zl, a2], axis=0).astype(BF16)
    g2_p = jnp.concatenate([g2, jnp.zeros((RWKV_XG_P - RWKV_GATE_LORA, RWKV_WIDTH), F32)],
                           axis=0).astype(BF16)
    params = (mu_p, row(w0), w2_p, row(a0), a2_p, g2_p, row(k_k), row(k_a), row(r_k), row(ln_w), row(ln_b))
    return pl.pallas_call(
        _rwkv_kernel,
        grid=(batch, nt),
        in_specs=[pl.BlockSpec((SEQ_TILE, RWKV_P), lambda b, i: (b * nt + i, 0))]
                 + [_resident(v.shape) for v in params],
        out_specs=pl.BlockSpec((SEQ_TILE, RWKV_WIDTH), lambda b, i: (b * nt + i, 0)),
        out_shape=jax.ShapeDtypeStruct((t, RWKV_WIDTH), BF16),
        scratch_shapes=[pltpu.VMEM((PAD_ROWS + SEQ_TILE, RWKV_P), F32),
                        pltpu.VMEM((SEQ_TILE, RWKV_WIDTH), F32),
                        pltpu.VMEM((RWKV_HEADS // 2, LANES, LANES), F32)],
        compiler_params=_cparams("parallel", "arbitrary"),
        name="rwkv7_time_mix",
    )(p, *params)


def _rwkv_weights(w_in_l):
    w = w_in_l[:, OFF_RWKV:OFF_GDN]
    return jnp.concatenate([w, jnp.zeros((D_MODEL, RWKV_XG_P - RWKV_GATE_LORA), F32)], axis=1).astype(BF16)


MOE_FF_TILE = FF_EXPERT // 2
assert MOE_FF_TILE % MXU_COLS == 0


def _router_kernel(x_ref, g_ref, wr_ref, h_ref, idx_ref, wt_ref):
    h = _rms(x_ref[...], g_ref[...])
    h_ref[...] = h
    wr = wr_ref[...]
    h_hi = h.astype(BF16)
    h_lo = (h - h_hi.astype(F32)).astype(BF16)
    w_hi = wr.astype(BF16)
    w_lo = (wr - w_hi.astype(F32)).astype(BF16)
    logits = (jnp.dot(h_hi, w_hi, preferred_element_type=F32)
              + jnp.dot(h_lo, w_hi, preferred_element_type=F32)
              + jnp.dot(h_hi, w_lo, preferred_element_type=F32))
    lane = _iota2(logits.shape, 1)
    logits = jnp.where(lane < N_EXPERTS, logits, -jnp.inf)
    m1 = jnp.max(logits, axis=-1, keepdims=True)
    i1 = jnp.min(jnp.where(logits == m1, lane, LANES), axis=-1, keepdims=True)
    rest = jnp.where(lane == i1, -jnp.inf, logits)
    m2 = jnp.max(rest, axis=-1, keepdims=True)
    i2 = jnp.min(jnp.where(rest == m2, lane, LANES), axis=-1, keepdims=True)
    e = jnp.exp(m2 - m1)
    w1 = 1.0 / (1.0 + e)
    idx_ref[...] = jnp.where(lane == 0, i1, i2)
    wt_ref[...] = jnp.where(lane == 0, w1, e * w1)


def moe_route(x, g, router):
    t = x.shape[0]
    wr = jnp.concatenate([router.astype(F32), jnp.zeros((D_MODEL, LANES - N_EXPERTS), F32)], axis=1)
    return pl.pallas_call(
        _router_kernel,
        grid=(t // ROW_TILE,),
        in_specs=[_rows(ROW_TILE, D_MODEL), _resident((1, D_MODEL)), _resident(wr.shape)],
        out_specs=[_rows(ROW_TILE, D_MODEL), _rows(ROW_TILE, LANES), _rows(ROW_TILE, LANES)],
        out_shape=[jax.ShapeDtypeStruct((t, D_MODEL), F32), jax.ShapeDtypeStruct((t, LANES), jnp.int32),
                   jax.ShapeDtypeStruct((t, LANES), F32)],
        compiler_params=_cparams("parallel"),
        name="moe_router",
    )(x, g, wr)


DMA_UNROLL = 8


def _moe_plan(idx):
    t = idx.shape[0]
    n_pairs = t * TOP_K
    n_blk = -(-(n_pairs + N_EXPERTS * (MOE_BLOCK - 1)) // MOE_BLOCK)
    n_slot = n_blk * MOE_BLOCK
    flat_e = idx[:, :TOP_K].reshape(n_pairs)
    onehot = (flat_e[:, None] == jnp.arange(N_EXPERTS, dtype=jnp.int32)[None, :]).astype(jnp.int32)
    rank = jnp.sum((jnp.cumsum(onehot, axis=0) - onehot) * onehot, axis=1)
    counts = jnp.sum(onehot, axis=0)
    padded = (counts + MOE_BLOCK - 1) // MOE_BLOCK * MOE_BLOCK
    pad_end = jnp.cumsum(padded)
    pad_start = pad_end - padded
    dest = pad_start[flat_e] + rank
    pair_id = jnp.arange(n_pairs, dtype=jnp.int32)
    pair_row = (pair_id % TOP_K) * t + pair_id // TOP_K
    spare = n_pairs + jnp.arange(n_slot, dtype=jnp.int32) % MOE_BLOCK
    slot_dst = spare.at[dest].set(pair_row)
    slot_tok = jnp.where(slot_dst < n_pairs, slot_dst % t, 0)
    blk_start = jnp.arange(n_blk, dtype=jnp.int32) * MOE_BLOCK
    blk_e = jnp.minimum(jnp.sum((pad_end[None, :] <= blk_start[:, None]).astype(jnp.int32), axis=1),
                        N_EXPERTS - 1)
    return slot_tok, slot_dst, blk_e


def _moe_kernel(blk_e_ref, tok_ref, tokn_ref, dst_ref, dstp_ref, h_hbm, wg_ref, wu_ref, wd_ref, out_hbm,
                xbuf, xb_ref, acc_ref, ybuf, gsem, ssem):
    n = pl.program_id(0)
    f = pl.program_id(1)
    n_blk = pl.num_programs(0)
    n_ff = FF_EXPERT // MOE_FF_TILE
    rows_per_step = MOE_BLOCK // n_ff
    slot = n % 2

    def row_in(tok, q, j, s):
        return pltpu.make_async_copy(h_hbm.at[pl.ds(tok, 1), :], xbuf.at[s, q, pl.ds(j, 1), :], gsem.at[s])

    def row_out(q, j, d):
        return pltpu.make_async_copy(ybuf.at[q, pl.ds(j, 1), :], out_hbm.at[pl.ds(d, 1), :], ssem.at[0])

    def for_rows(fn):
        def body(r, carry):
            fn(r, r // rows_per_step, r % rows_per_step)
            return carry
        lax.fori_loop(0, MOE_BLOCK, body, 0, unroll=DMA_UNROLL)

    @pl.when(f == 0)
    def _():
        @pl.when(n == 0)
        def _():
            for_rows(lambda r, q, j: row_in(tok_ref[0, 0, r], q, j, 0).start())
            ybuf[...] = jnp.zeros(ybuf.shape, F32)

        for_rows(lambda r, q, j: row_in(tok_ref[0, 0, r], q, j, slot).wait())
        xb_ref[...] = xbuf[slot].reshape(MOE_BLOCK, D_MODEL).astype(BF16)

    r0 = f * rows_per_step
    for j in range(rows_per_step):
        row_in(tokn_ref[0, 0, r0 + j], f, j, 1 - slot).start()
        row_out(f, j, dstp_ref[0, 0, r0 + j]).start()

    xb = xb_ref[...]
    gate = jnp.dot(xb, wg_ref[0], preferred_element_type=F32)
    up = jnp.dot(xb, wu_ref[0], preferred_element_type=F32)
    part = jnp.dot((_silu(gate) * up).astype(BF16), wd_ref[0], preferred_element_type=F32)

    @pl.when(f == 0)
    def _():
        acc_ref[...] = part

    @pl.when(jnp.logical_and(f > 0, f < n_ff - 1))
    def _():
        acc_ref[...] += part

    @pl.when(f == n_ff - 1)
    def _():
        for_rows(lambda r, q, j: row_out(q, j, dstp_ref[0, 0, r]).wait())
        ybuf[...] = (acc_ref[...] + part).reshape(ybuf.shape)

        @pl.when(n == n_blk - 1)
        def _():
            for_rows(lambda r, q, j: row_out(q, j, dst_ref[0, 0, r]).start())
            for_rows(lambda r, q, j: row_out(q, j, dst_ref[0, 0, r]).wait())
            for_rows(lambda r, q, j: row_in(tokn_ref[0, 0, r], q, j, 1 - slot).wait())


def moe_experts(h, slot_tok, slot_dst, blk_e, w_gu, w_down):
    t = h.shape[0]
    n_blk = blk_e.shape[0]
    n_ff = FF_EXPERT // MOE_FF_TILE
    tok3 = slot_tok.reshape(n_blk, 1, MOE_BLOCK)
    dst3 = slot_dst.reshape(n_blk, 1, MOE_BLOCK)
    smem_blk = lambda imap: pl.BlockSpec((1, 1, MOE_BLOCK), imap, memory_space=pltpu.SMEM)
    grid_spec = pltpu.PrefetchScalarGridSpec(
        num_scalar_prefetch=1,
        grid=(n_blk, n_ff),
        in_specs=[
            smem_blk(lambda n, f, be: (n, 0, 0)),
            smem_blk(lambda n, f, be: (jnp.minimum(n + 1, be.shape[0] - 1), 0, 0)),
            smem_blk(lambda n, f, be: (n, 0, 0)),
            smem_blk(lambda n, f, be: (jnp.maximum(n - 1, 0), 0, 0)),
            pl.BlockSpec(memory_space=pl.ANY),
            pl.BlockSpec((1, D_MODEL, MOE_FF_TILE), lambda n, f, be: (be[n], 0, f)),
            pl.BlockSpec((1, D_MODEL, MOE_FF_TILE), lambda n, f, be: (be[n], 0, f + FF_EXPERT // MOE_FF_TILE)),
            pl.BlockSpec((1, MOE_FF_TILE, D_MODEL), lambda n, f, be: (be[n], f, 0)),
        ],
        out_specs=pl.BlockSpec(memory_space=pl.ANY),
        scratch_shapes=[pltpu.VMEM((2, n_ff, MOE_BLOCK // n_ff, D_MODEL), F32),
                        pltpu.VMEM((MOE_BLOCK, D_MODEL), BF16),
                        pltpu.VMEM((MOE_BLOCK, D_MODEL), F32),
                        pltpu.VMEM((n_ff, MOE_BLOCK // n_ff, D_MODEL), F32),
                        pltpu.SemaphoreType.DMA((2,)),
                        pltpu.SemaphoreType.DMA((1,))],
    )
    return pl.pallas_call(
        _moe_kernel,
        grid_spec=grid_spec,
        out_shape=jax.ShapeDtypeStruct((TOP_K * t + MOE_BLOCK, D_MODEL), F32),
        compiler_params=_cparams("arbitrary", "arbitrary"),
        name="moe_experts",
    )(blk_e, tok3, tok3, dst3, dst3, h, w_gu, w_gu, w_down)


def _moe_combine_kernel(x_ref, w_ref, y0_ref, y1_ref, *rest):
    w = w_ref[...]
    y = x_ref[...] + w[:, 0:1] * y0_ref[...] + w[:, 1:2] * y1_ref[...]
    if len(rest) == 2:
        g_ref, o_ref = rest
        o_ref[...] = _rms(y, g_ref[...])
    else:
        rest[0][...] = y


def moe_combine(x, wts, pair_out, out_norm=None):
    t = x.shape[0]
    nt = t // ROW_TILE
    extra = () if out_norm is None else (out_norm,)
    return pl.pallas_call(
        _moe_combine_kernel,
        grid=(nt,),
        in_specs=[_rows(ROW_TILE, D_MODEL), _rows(ROW_TILE, LANES), _rows(ROW_TILE, D_MODEL),
                  pl.BlockSpec((ROW_TILE, D_MODEL), lambda i: (i + nt, 0))]
                 + [_resident(v.shape) for v in extra],
        out_specs=_rows(ROW_TILE, D_MODEL),
        out_shape=jax.ShapeDtypeStruct((t, D_MODEL), F32),
        compiler_params=_cparams("parallel"),
        name="moe_combine",
    )(x, wts, pair_out, pair_out, *extra)


def moe_swiglu(x, g, router, w_gu, w_down, out_norm=None):
    h, idx, wts = moe_route(x, g, router)
    slot_tok, slot_dst, blk_e = _moe_plan(idx)
    pair_out = moe_experts(h, slot_tok, slot_dst, blk_e, w_gu, w_down)
    return moe_combine(x, wts, pair_out, out_norm)


def kernel(x, positions, norm_mix, w_in, mla_q_norm, mla_kv_norm, mla_w_uq, mla_w_ukv, rwkv_mu, rwkv_w0,
           rwkv_w2, rwkv_a0, rwkv_a2, rwkv_g2, rwkv_k_k, rwkv_k_a, rwkv_r_k, rwkv_ln_w, rwkv_ln_b, gdn_conv,
           gdn_a_log, gdn_dt_bias, gdn_norm, ssd_conv_w, ssd_conv_b, ssd_dt_bias, ssd_a_log, ssd_d, ssd_norm,
           gate_b, w_branch, w_out, norm_ffn, ffn_w_gu, ffn_w_down, moe_router, moe_w_gu, moe_w_down,
           norm_final):
    batch, seq, d = x.shape
    t = batch * seq
    depth = w_in.shape[0]
    xf = x.reshape(t, d)
    cos, sin = rope_tables(positions.reshape(t, 1).astype(jnp.int32))
    row = lambda v: v.reshape(1, -1)
    for layer in range(depth):
        w_l = w_in[layer]
        w_a, wq, wqr, wk, wv = _mla_weights(w_l, mla_w_uq[layer], mla_w_ukv[layer])
        pa, pb, pc, pd = norm_proj(xf, row(norm_mix[layer]),
                                   [w_a, _rwkv_weights(w_l), _gdn_weights(w_l), _ssd_weights(w_l)])
        y_a = mla_attention(pa, cos, sin, row(mla_q_norm[layer]), row(mla_kv_norm[layer]),
                            wq, wqr, wk, wv, batch, seq)
        y_b = rwkv7_time_mix(pb, rwkv_mu[layer], rwkv_w0[layer], rwkv_w2[layer], rwkv_a0[layer],
                             rwkv_a2[layer], rwkv_g2[layer], rwkv_k_k[layer], rwkv_k_a[layer],
                             rwkv_r_k[layer], rwkv_ln_w[layer], rwkv_ln_b[layer], batch, seq)
        y_c = gated_deltanet(pc, gdn_conv[layer], gdn_a_log[layer], gdn_dt_bias[layer], gdn_norm[layer],
                             batch, seq)
        y_d = mamba2_ssd(pd, ssd_conv_w[layer], ssd_conv_b[layer], ssd_dt_bias[layer], ssd_a_log[layer],
                         ssd_d[layer], ssd_norm[layer], batch, seq)
        w_gate = w_l[:, OFF_GATE:].reshape(D_MODEL, N_BRANCH, D_MODEL).transpose(1, 0, 2).astype(BF16)
        xf = merge(xf, row(norm_mix[layer]), w_gate, gate_b[layer], (y_a, y_b, y_c, y_d),
                   w_branch[layer].astype(BF16), w_out[layer].astype(BF16))
        last = layer == depth - 1
        if layer % 2 == 0:
            xf = ffn_dense(xf, row(norm_ffn[layer]), ffn_w_gu[layer // 2].astype(BF16),
                           ffn_w_down[layer // 2].astype(BF16))
            if last:
                xf = final_norm(xf, row(norm_final))
        else:
            xf = moe_swiglu(xf, row(norm_ffn[layer]), moe_router[layer // 2],
                            moe_w_gu[layer // 2].astype(BF16), moe_w_down[layer // 2].astype(BF16),
                            out_norm=row(norm_final) if last else None)
    return xf.reshape(batch, seq, d)
```

```python
import functools
import math

import jax
import jax.numpy as jnp
from jax import lax
from jax.experimental import pallas as pl
from jax.experimental.pallas import tpu as pltpu

F32 = jnp.float32
BF16 = jnp.bfloat16

D_MODEL = 1024
CHUNK = 64
EPS = 1e-6
MLA_HEADS = 4
MLA_Q_LORA = 384
MLA_KV_LORA = 256
MLA_NOPE = 128
MLA_ROPE = 64
MLA_V = 128
ROPE_THETA = 10000.0
RWKV_HEADS = 8
RWKV_HEAD = 64
RWKV_WIDTH = RWKV_HEADS * RWKV_HEAD
RWKV_DECAY_LORA = 64
RWKV_A_LORA = 64
RWKV_GATE_LORA = 160
RWKV_GN_EPS = 64e-5
GDN_HEADS = 4
GDN_HEAD = 128
GDN_WIDTH = GDN_HEADS * GDN_HEAD
CONV_K = 4
SSD_HEADS = 8
SSD_HEAD = 64
SSD_WIDTH = SSD_HEADS * SSD_HEAD
SSD_GROUPS = 2
SSD_STATE = 128
SSD_CONV_DIM = SSD_WIDTH + 2 * SSD_GROUPS * SSD_STATE
N_BRANCH = 4
BRANCH_WIDTH = 512
FF_DENSE = 2816
N_EXPERTS = 8
TOP_K = 2
FF_EXPERT = 3584
MOE_BLOCK = 512

MLA_COLS = MLA_Q_LORA + MLA_KV_LORA + MLA_ROPE
RWKV_COLS = 3 * RWKV_WIDTH + RWKV_DECAY_LORA + RWKV_A_LORA + RWKV_GATE_LORA
GDN_COLS = 4 * GDN_WIDTH + 2 * GDN_HEADS
SSD_COLS = SSD_WIDTH + SSD_CONV_DIM + SSD_HEADS
OFF_RWKV = MLA_COLS
OFF_GDN = OFF_RWKV + RWKV_COLS
OFF_SSD = OFF_GDN + GDN_COLS
OFF_GATE = OFF_SSD + SSD_COLS

LANES = 128
SUBLANES = 8
VMEM_LIMIT_BYTES = 56 * 2**20

ROW_TILE = 256
SEQ_TILE = 256
ATT_TILE = 256
PAD_ROWS = SUBLANES

MLA_P = MLA_Q_LORA + MLA_KV_LORA + 2 * LANES
RWKV_P = 3 * RWKV_WIDTH + LANES + 2 * LANES
GDN_P = 4 * GDN_WIDTH + LANES
SSD_P = SSD_WIDTH + SSD_CONV_DIM + LANES


def _cparams(*sem):
    return pltpu.CompilerParams(dimension_semantics=sem, vmem_limit_bytes=VMEM_LIMIT_BYTES)


def _resident(shape):
    nd = len(shape)
    return pl.BlockSpec(shape, lambda *_: (0,) * nd, pipeline_mode=pl.Buffered(1))


def _rows(tile, width):
    return pl.BlockSpec((tile, width), lambda i: (i, 0))


def _bdot(a, b):
    return jnp.dot(a.astype(BF16), b.astype(BF16), preferred_element_type=F32)


def _bdot_nt(a, b):
    return lax.dot_general(a.astype(BF16), b.astype(BF16), (((1,), (1,)), ((), ())),
                           preferred_element_type=F32)


def _bdot_tn(a, b):
    return lax.dot_general(a.astype(BF16), b.astype(BF16), (((0,), (0,)), ((), ())),
                           preferred_element_type=F32)


def _split_dot(m01, x):
    hi = x.astype(BF16)
    lo = (x - hi.astype(F32)).astype(BF16)
    return (jnp.dot(m01, hi, preferred_element_type=F32)
            + jnp.dot(m01, lo, preferred_element_type=F32))


def _dot_split(x, m01):
    hi = x.astype(BF16)
    lo = (x - hi.astype(F32)).astype(BF16)
    return (jnp.dot(hi, m01, preferred_element_type=F32)
            + jnp.dot(lo, m01, preferred_element_type=F32))


def _rms(x, g):
    return x * lax.rsqrt(jnp.mean(x * x, axis=-1, keepdims=True) + EPS) * g


def _sigmoid(x):
    return 1.0 / (1.0 + jnp.exp(-x))


def _silu(x):
    return x * _sigmoid(x)


def _softplus(x):
    return jnp.maximum(x, 0.0) + jnp.log(1.0 + jnp.exp(-jnp.abs(x)))


def _iota2(shape, dim):
    return lax.broadcasted_iota(jnp.int32, shape, dim)


def _tri_inv(lows, steps):
    n = lows[0].shape[0]
    eye = (_iota2((n, n), 0) == _iota2((n, n), 1)).astype(F32)
    ps = [eye - low for low in lows]
    qs = [_bdot(low, low) for low in lows]
    for _ in range(steps - 2):
        rs = [_bdot(jnp.concatenate([p, q], axis=0), q) for p, q in zip(ps, qs)]
        ps = [p + r[:n] for p, r in zip(ps, rs)]
        qs = [r[n:] for r in rs]
    return [p + _bdot(p, q) for p, q in zip(ps, qs)]


MXU_COLS = 2 * LANES


def _norm_proj_kernel(x_ref, g_ref, w1_ref, w2_ref, oa_ref, ob_ref, oc_ref, od_ref):
    h = _rms(x_ref[...], g_ref[...]).astype(BF16)
    for w_ref, (p_ref, q_ref) in ((w1_ref, (oa_ref, od_ref)), (w2_ref, (ob_ref, oc_ref))):
        res = jnp.dot(h, w_ref[...], preferred_element_type=F32)
        n = p_ref.shape[1]
        p_ref[...] = res[:, :n]
        q_ref[...] = res[:, n:]


def norm_proj(x, g, weights):
    t = x.shape[0]
    wa, wb, wc, wd = weights
    widths = [w.shape[1] for w in weights]
    w1 = jnp.concatenate([wa, wd], axis=1)
    w2 = jnp.concatenate([wb, wc], axis=1)
    assert w1.shape[1] % MXU_COLS == 0 and w2.shape[1] % MXU_COLS == 0
    return pl.pallas_call(
        _norm_proj_kernel,
        grid=(t // ROW_TILE,),
        in_specs=[_rows(ROW_TILE, D_MODEL), _resident((1, D_MODEL)), _resident(w1.shape), _resident(w2.shape)],
        out_specs=[_rows(ROW_TILE, n) for n in widths],
        out_shape=[jax.ShapeDtypeStruct((t, n), F32) for n in widths],
        compiler_params=_cparams("parallel"),
        name="norm_proj",
    )(x, g, w1, w2)


def _rope_table_kernel(pos_ref, freq_ref, cos_ref, sin_ref):
    ang = pos_ref[...].astype(F32) * freq_ref[...]
    cos_ref[...] = jnp.cos(ang)
    sin_ref[...] = jnp.sin(ang)


def rope_tables(pos_col):
    t = pos_col.shape[0]
    half = MLA_ROPE // 2
    inv_freq = ROPE_THETA ** (-jnp.arange(half, dtype=F32) / half)
    freq = jnp.concatenate([inv_freq, inv_freq, jnp.zeros((LANES - MLA_ROPE,), F32)])[None, :]
    tile = min(1024, t)
    return pl.pallas_call(
        _rope_table_kernel,
        grid=(t // tile,),
        in_specs=[_rows(tile, 1), _resident((1, LANES))],
        out_specs=[_rows(tile, LANES), _rows(tile, LANES)],
        out_shape=[jax.ShapeDtypeStruct((t, LANES), F32)] * 2,
        compiler_params=_cparams("parallel"),
        name="rope_tables",
    )(pos_col, freq)


def _mla_prep_kernel(p_ref, cos_ref, sin_ref, qn_ref, kvn_ref, wq_ref, wqr_ref, wk_ref, wv_ref,
                     q_ref, k_ref, v_ref):
    cos = cos_ref[...]
    sin = sin_ref[...]
    scale = (MLA_NOPE + MLA_ROPE) ** -0.5
    nq = _rms(p_ref[:, :MLA_Q_LORA], qn_ref[...]).astype(BF16)
    q = jnp.dot(nq, wq_ref[...], preferred_element_type=F32)
    qr = jnp.dot(nq, wqr_ref[...], preferred_element_type=F32)
    nkv = _rms(p_ref[:, MLA_Q_LORA:MLA_Q_LORA + MLA_KV_LORA], kvn_ref[...]).astype(BF16)
    kn = jnp.dot(nkv, wk_ref[...], preferred_element_type=F32)
    v_ref[...] = jnp.dot(nkv, wv_ref[...], preferred_element_type=F32).astype(BF16)
    o0 = MLA_Q_LORA + MLA_KV_LORA
    k_rope = (p_ref[:, o0:o0 + LANES] * cos + p_ref[:, o0 + LANES:o0 + 2 * LANES] * sin).astype(BF16)
    for h in range(MLA_HEADS):
        c0 = 2 * LANES * h
        q_ref[:, c0:c0 + LANES] = (q[:, c0:c0 + LANES] * scale).astype(BF16)
        q_rope = q[:, c0 + LANES:c0 + 2 * LANES] * cos + qr[:, h * LANES:(h + 1) * LANES] * sin
        q_ref[:, c0 + LANES:c0 + 2 * LANES] = (q_rope * scale).astype(BF16)
        k_ref[:, c0:c0 + LANES] = kn[:, h * LANES:(h + 1) * LANES].astype(BF16)
        k_ref[:, c0 + LANES:c0 + 2 * LANES] = k_rope


def _mla_attn_kernel(q_ref, k_ref, v_ref, o_ref):
    i = pl.program_id(1)
    tq = ATT_TILE
    hq = 2 * LANES
    heads = range(MLA_HEADS)
    qs = [q_ref[:, h * hq:(h + 1) * hq] for h in heads]

    def tiles(js, carry, masks):
        rows = [pl.ds(pl.multiple_of(j * ATT_TILE, ATT_TILE), ATT_TILE) for j in js]
        ss = [[lax.dot_general(qs[h], k_ref[r, h * hq:(h + 1) * hq], (((1,), (1,)), ((), ())),
                               preferred_element_type=F32) for r in rows] for h in heads]
        ss = [[s if m is None else jnp.where(m, s, -jnp.inf) for s, m in zip(ss[h], masks)] for h in heads]
        m_new = []
        for h in heads:
            m = carry[h][0]
            for s in ss[h]:
                m = jnp.maximum(m, jnp.max(s, axis=-1, keepdims=True))
            m_new.append(m)
        alpha = [jnp.exp(carry[h][0] - m_new[h]) for h in heads]
        ps = [[jnp.exp(s - m_new[h]) for s in ss[h]] for h in heads]
        pv = [jnp.dot(jnp.concatenate([p.astype(BF16) for p in ps[h]], axis=1),
                      jnp.concatenate([v_ref[r, h * MLA_V:(h + 1) * MLA_V] for r in rows], axis=0),
                      preferred_element_type=F32) for h in heads]
        out = []
        for h in heads:
            l = carry[h][1] * alpha[h]
            for p in ps[h]:
                l = l + jnp.sum(p, axis=-1, keepdims=True)
            out.append((m_new[h], l, carry[h][2] * alpha[h] + pv[h]))
        return tuple(out)

    init = tuple((jnp.full((tq, 1), -jnp.inf, F32), jnp.zeros((tq, 1), F32), jnp.zeros((tq, MLA_V), F32))
                 for _ in heads)
    carry = lax.fori_loop(0, i // 2, lambda p, c: tiles((2 * p, 2 * p + 1), c, (None, None)), init)
    pending = jnp.broadcast_to(i % 2 == 1, (tq, ATT_TILE))
    allowed = (_iota2((tq, ATT_TILE), 1) // CHUNK) <= (_iota2((tq, ATT_TILE), 0) // CHUNK)
    final = tiles((jnp.maximum(i - 1, 0), i), carry, (pending, allowed))
    for h in heads:
        o_ref[:, h * MLA_V:(h + 1) * MLA_V] = (final[h][2] / final[h][1]).astype(o_ref.dtype)


def mla_attention(p, cos, sin, q_norm, kv_norm, wq, wqr, wk, wv, batch, seq):
    t = p.shape[0]
    hq = 2 * LANES
    q, k, v = pl.pallas_call(
        _mla_prep_kernel,
        grid=(t // ROW_TILE,),
        in_specs=[_rows(ROW_TILE, MLA_P), _rows(ROW_TILE, LANES), _rows(ROW_TILE, LANES),
                  _resident(q_norm.shape), _resident(kv_norm.shape), _resident(wq.shape),
                  _resident(wqr.shape), _resident(wk.shape), _resident(wv.shape)],
        out_specs=[_rows(ROW_TILE, MLA_HEADS * hq), _rows(ROW_TILE, MLA_HEADS * hq),
                   _rows(ROW_TILE, MLA_HEADS * MLA_V)],
        out_shape=[jax.ShapeDtypeStruct((t, MLA_HEADS * hq), BF16),
                   jax.ShapeDtypeStruct((t, MLA_HEADS * hq), BF16),
                   jax.ShapeDtypeStruct((t, MLA_HEADS * MLA_V), BF16)],
        compiler_params=_cparams("parallel"),
        name="mla_prep",
    )(p, cos, sin, q_norm, kv_norm, wq, wqr, wk, wv)
    nq = seq // ATT_TILE
    return pl.pallas_call(
        _mla_attn_kernel,
        grid=(batch, nq),
        in_specs=[pl.BlockSpec((ATT_TILE, MLA_HEADS * hq), lambda b, i: (b * nq + i, 0)),
                  pl.BlockSpec((seq, MLA_HEADS * hq), lambda b, i: (b, 0)),
                  pl.BlockSpec((seq, MLA_HEADS * MLA_V), lambda b, i: (b, 0))],
        out_specs=pl.BlockSpec((ATT_TILE, MLA_HEADS * MLA_V), lambda b, i: (b * nq + i, 0)),
        out_shape=jax.ShapeDtypeStruct((t, MLA_HEADS * MLA_V), BF16),
        compiler_params=_cparams("parallel", "arbitrary"),
        name="mla_attn",
    )(q, k, v)


def _mla_weights(w_in_l, w_uq, w_ukv):
    half = MLA_ROPE // 2
    zpad = lambda rows, n: jnp.zeros((rows, n), F32)
    o0 = MLA_Q_LORA + MLA_KV_LORA
    kr = w_in_l[:, o0:o0 + MLA_ROPE]
    kr_rot = jnp.concatenate([-kr[:, half:], kr[:, :half]], axis=1)
    w_a = jnp.concatenate([w_in_l[:, :o0], kr, zpad(D_MODEL, LANES - MLA_ROPE),
                           kr_rot, zpad(D_MODEL, LANES - MLA_ROPE)], axis=1)
    uq = w_uq.reshape(MLA_Q_LORA, MLA_HEADS, MLA_NOPE + MLA_ROPE)
    rope_w = uq[:, :, MLA_NOPE:]
    wq = jnp.concatenate([uq, jnp.zeros((MLA_Q_LORA, MLA_HEADS, LANES - MLA_ROPE), F32)], axis=2)
    wqr = jnp.concatenate([-rope_w[:, :, half:], rope_w[:, :, :half],
                           jnp.zeros((MLA_Q_LORA, MLA_HEADS, LANES - MLA_ROPE), F32)], axis=2)
    ukv = w_ukv.reshape(MLA_KV_LORA, MLA_HEADS, MLA_NOPE + MLA_V)
    wk = ukv[:, :, :MLA_NOPE].reshape(MLA_KV_LORA, MLA_HEADS * MLA_NOPE)
    wv = ukv[:, :, MLA_NOPE:].reshape(MLA_KV_LORA, MLA_HEADS * MLA_V)
    return (w_a.astype(BF16), wq.reshape(MLA_Q_LORA, -1).astype(BF16),
            wqr.reshape(MLA_Q_LORA, -1).astype(BF16), wk.astype(BF16), wv.astype(BF16))


def _merge_kernel(x_ref, g_ref, wg_ref, gb_ref, ya_ref, yb_ref, yc_ref, yd_ref, wb_ref, wo_ref, o_ref):
    x = x_ref[...]
    h = _rms(x, g_ref[...]).astype(BF16)
    merged = None
    for i, y_ref in enumerate((ya_ref, yb_ref, yc_ref, yd_ref)):
        gate = _sigmoid(jnp.dot(h, wg_ref[i], preferred_element_type=F32) + gb_ref[i:i + 1, :])
        term = gate * jnp.dot(y_ref[...], wb_ref[i], preferred_element_type=F32)
        merged = term if merged is None else merged + term
    o_ref[...] = x + jnp.dot(merged.astype(BF16), wo_ref[...], preferred_element_type=F32)


def merge(x, g, w_gate, gate_b, ys, w_branch, w_out):
    t = x.shape[0]
    return pl.pallas_call(
        _merge_kernel,
        grid=(t // ROW_TILE,),
        in_specs=[_rows(ROW_TILE, D_MODEL), _resident((1, D_MODEL)), _resident(w_gate.shape),
                  _resident(gate_b.shape)] + [_rows(ROW_TILE, BRANCH_WIDTH)] * N_BRANCH
                 + [_resident(w_branch.shape), _resident(w_out.shape)],
        out_specs=_rows(ROW_TILE, D_MODEL),
        out_shape=jax.ShapeDtypeStruct((t, D_MODEL), F32),
        compiler_params=_cparams("parallel"),
        name="merge",
    )(x, g, w_gate, gate_b, *ys, w_branch, w_out)


FF_SPLIT = (FF_DENSE // MXU_COLS // 2) * MXU_COLS


def _ffn_kernel(x_ref, g_ref, wgu_ref, wd_ref, o_ref):
    x = x_ref[...]
    h = _rms(x, g_ref[...]).astype(BF16)
    acc = x
    for c0, c1 in ((0, FF_SPLIT), (FF_SPLIT, FF_DENSE)):
        gate = jnp.dot(h, wgu_ref[:, c0:c1], preferred_element_type=F32)
        up = jnp.dot(h, wgu_ref[:, FF_DENSE + c0:FF_DENSE + c1], preferred_element_type=F32)
        act = (_silu(gate) * up).astype(BF16)
        acc = acc + jnp.dot(act, wd_ref[c0:c1, :], preferred_element_type=F32)
    o_ref[...] = acc


def ffn_dense(x, g, w_gu, w_down):
    t = x.shape[0]
    return pl.pallas_call(
        _ffn_kernel,
        grid=(t // ROW_TILE,),
        in_specs=[_rows(ROW_TILE, D_MODEL), _resident((1, D_MODEL)), _resident(w_gu.shape),
                  _resident(w_down.shape)],
        out_specs=_rows(ROW_TILE, D_MODEL),
        out_shape=jax.ShapeDtypeStruct((t, D_MODEL), F32),
        compiler_params=_cparams("parallel"),
        name="ffn_dense",
    )(x, g, w_gu, w_down)


def _final_norm_kernel(x_ref, g_ref, o_ref):
    o_ref[...] = _rms(x_ref[...], g_ref[...])


def final_norm(x, g):
    t = x.shape[0]
    tile = min(1024, t)
    return pl.pallas_call(
        _final_norm_kernel,
        grid=(t // tile,),
        in_specs=[_rows(tile, D_MODEL), _resident((1, D_MODEL))],
        out_specs=_rows(tile, D_MODEL),
        out_shape=jax.ShapeDtypeStruct((t, D_MODEL), F32),
        compiler_params=_cparams("parallel"),
        name="final_norm",
    )(x, g)


def _causal_conv(buf_ref, x, w_ref):
    tile = x.shape[0]
    buf_ref[PAD_ROWS:PAD_ROWS + tile, :] = x
    xx = buf_ref[0:PAD_ROWS + tile, :]
    acc = x * w_ref[CONV_K - 1:CONV_K, :]
    for j in range(CONV_K - 1):
        shift = CONV_K - 1 - j
        acc = acc + pltpu.roll(xx, shift, axis=0)[PAD_ROWS:] * w_ref[j:j + 1, :]
    buf_ref[0:PAD_ROWS, :] = buf_ref[tile:tile + PAD_ROWS, :]
    return acc


def _chunk_tri(tile):
    r = _iota2((tile, tile), 0)
    c = _iota2((tile, tile), 1)
    return jnp.where(c <= r, jnp.where(r // CHUNK == c // CHUNK, 1.0, 0.0), 0.0).astype(BF16)


def _head_expand(width):
    n = LANES * width
    return jnp.where(_iota2((LANES, n), 1) // width == _iota2((LANES, n), 0), 1.0, 0.0).astype(BF16)


def _ssd_kernel(p_ref, cw_ref, cb_ref, dtb_ref, a_ref, d_ref, nw_ref, o_ref, buf_ref, y_ref, h_ref):
    ts = SEQ_TILE
    c_len = CHUNK
    w = SSD_WIDTH
    gn = SSD_GROUPS * SSD_STATE

    @pl.when(pl.program_id(1) == 0)
    def _():
        buf_ref[0:PAD_ROWS, :] = jnp.zeros((PAD_ROWS, SSD_CONV_DIM), F32)
        h_ref[...] = jnp.zeros(h_ref.shape, F32)

    z = p_ref[:, :w]
    xbc = _silu(_causal_conv(buf_ref, p_ref[:, w:w + SSD_CONV_DIM], cw_ref) + cb_ref[...])
    x = xbc[:, :w]
    bm = xbc[:, w:w + gn]
    cm = xbc[:, w + gn:]
    dt = _softplus(p_ref[:, w + SSD_CONV_DIM:] + dtb_ref[...])
    a = dt * a_ref[...]
    acum = _split_dot(_chunk_tri(ts), a)
    expand = _head_expand(SSD_HEAD)[:, :w]
    dt_x = _dot_split(dt, expand)
    acum_x = _dot_split(acum, expand)
    xdt = x * dt_x
    acum_next = pltpu.roll(acum, LANES - 1, axis=1)
    lane = _iota2((c_len, LANES), 1)
    left = lane < SSD_HEAD
    causal2 = (lane % c_len) <= _iota2((c_len, LANES), 0)

    for c in range(ts // c_len):
        r0 = c * c_len
        ac = acum[r0:r0 + c_len]
        at = jnp.concatenate([ac, acum_next[r0:r0 + c_len]], axis=0).T
        ax = acum_x[r0:r0 + c_len]
        a_last = ax[c_len - 1:c_len, :]
        e_in = jnp.exp(ax)
        xdt_c = xdt[r0:r0 + c_len]
        xdt_d = xdt_c * jnp.exp(a_last - ax)
        e_last = jnp.exp(a_last)
        for g in range(SSD_GROUPS):
            bm_g = bm[r0:r0 + c_len, g * SSD_STATE:(g + 1) * SSD_STATE]
            cm_g = cm[r0:r0 + c_len, g * SSD_STATE:(g + 1) * SSD_STATE]
            cb2 = _bdot_nt(cm_g, jnp.concatenate([bm_g, bm_g], axis=0))
            for pp in range(SSD_HEADS // SSD_GROUPS // 2):
                p = g * (SSD_HEADS // SSD_GROUPS // 2) + pp
                l0 = p * LANES
                col2 = jnp.where(left, ac[:, 2 * p:2 * p + 1], ac[:, 2 * p + 1:2 * p + 2])
                row2 = at[2 * p:2 * p + 1, :]
                seg = jnp.exp(jnp.where(causal2, col2 - row2, -jnp.inf))
                xp = xdt_c[:, l0:l0 + LANES]
                xs = jnp.concatenate([jnp.where(left, xp, 0.0), jnp.where(left, 0.0, xp)], axis=0)
                y_diag = _bdot(cb2 * seg, xs)
                hp = h_ref[p]
                y_off = _bdot(cm_g, hp) * e_in[:, l0:l0 + LANES]
                h_ref[p] = hp * e_last[:, l0:l0 + LANES] + _bdot_tn(bm_g, xdt_d[:, l0:l0 + LANES])
                y_ref[r0:r0 + c_len, l0:l0 + LANES] = (
                    y_diag + y_off + d_ref[:, l0:l0 + LANES] * x[r0:r0 + c_len, l0:l0 + LANES])

    y = y_ref[...] * _silu(z)
    gw = w // SSD_GROUPS
    for g in range(SSD_GROUPS):
        yg = y[:, g * gw:(g + 1) * gw]
        o_ref[:, g * gw:(g + 1) * gw] = _rms(yg, nw_ref[:, g * gw:(g + 1) * gw]).astype(o_ref.dtype)


def mamba2_ssd(p, conv_w, conv_b, dt_bias, a_log, d_skip, norm_w, batch, seq):
    t = p.shape[0]
    nt = seq // SEQ_TILE
    pad = lambda v: jnp.concatenate([v.astype(F32), jnp.zeros((LANES - v.shape[0],), F32)])[None, :]
    a_row = pad(-jnp.exp(a_log.astype(F32)))
    d_row = jnp.repeat(d_skip.astype(F32), SSD_HEAD)[None, :]
    params = (conv_w, conv_b[None, :], pad(dt_bias), a_row, d_row, norm_w[None, :])
    return pl.pallas_call(
        _ssd_kernel,
        grid=(batch, nt),
        in_specs=[pl.BlockSpec((SEQ_TILE, SSD_P), lambda b, i: (b * nt + i, 0))]
                 + [_resident(v.shape) for v in params],
        out_specs=pl.BlockSpec((SEQ_TILE, SSD_WIDTH), lambda b, i: (b * nt + i, 0)),
        out_shape=jax.ShapeDtypeStruct((t, SSD_WIDTH), BF16),
        scratch_shapes=[pltpu.VMEM((PAD_ROWS + SEQ_TILE, SSD_CONV_DIM), F32),
                        pltpu.VMEM((SEQ_TILE, SSD_WIDTH), F32),
                        pltpu.VMEM((SSD_HEADS // 2, SSD_STATE, LANES), F32)],
        compiler_params=_cparams("parallel", "arbitrary"),
        name="mamba2_ssd",
    )(p, *params)


def _ssd_weights(w_in_l):
    w = w_in_l[:, OFF_SSD:OFF_GATE]
    return jnp.concatenate([w, jnp.zeros((D_MODEL, LANES - SSD_HEADS), F32)], axis=1).astype(BF16)


def _l2n(x):
    return x * lax.rsqrt(jnp.sum(x * x, axis=-1, keepdims=True) + EPS)


def _gdn_kernel(p_ref, cw_ref, bias_ref, arow_ref, nw_ref, o_ref, buf_ref, s_ref):
    ts = SEQ_TILE
    c_len = CHUNK
    w = GDN_WIDTH
    dh = GDN_HEAD

    @pl.when(pl.program_id(1) == 0)
    def _():
        buf_ref[0:PAD_ROWS, :] = jnp.zeros((PAD_ROWS, 3 * w), F32)
        s_ref[...] = jnp.zeros(s_ref.shape, F32)

    qkv = _silu(_causal_conv(buf_ref, p_ref[:, :3 * w], cw_ref))
    ba = p_ref[:, 4 * w:]
    beta = _sigmoid(ba)
    g = arow_ref[...] * _softplus(ba + bias_ref[...])
    gcum = _split_dot(_chunk_tri(ts), g)
    rr = _iota2((c_len, c_len), 0)
    cc = _iota2((c_len, c_len), 1)
    incl = cc <= rr
    strict = cc < rr

    n_chunks = ts // c_len
    chains = [(c, h) for c in range(n_chunks) for h in range(GDN_HEADS)]
    gts = []
    for c in range(n_chunks):
        gc_all = gcum[c * c_len:(c + 1) * c_len]
        gts.append(jnp.concatenate([gc_all, gc_all], axis=0).T)
    pre = []
    for c, h in chains:
        r0, l0 = c * c_len, h * dh
        q = _l2n(qkv[r0:r0 + c_len, l0:l0 + dh]) * dh ** -0.5
        k = _l2n(qkv[r0:r0 + c_len, w + l0:w + l0 + dh])
        v = qkv[r0:r0 + c_len, 2 * w + l0:2 * w + l0 + dh]
        b = beta[r0:r0 + c_len, h:h + 1]
        gcol = gcum[r0:r0 + c_len, GDN_HEADS + h:GDN_HEADS + h + 1]
        grow = gts[c][GDN_HEADS + h:GDN_HEADS + h + 1, :c_len]
        decay = jnp.exp(jnp.where(incl, gcol - grow, -jnp.inf))
        pre.append((q, k, v, b, gcol, decay, k * b))
    rs = [_bdot_nt(jnp.concatenate([kb, q], axis=0), k) for q, k, v, b, gcol, decay, kb in pre]
    lows = [jnp.where(strict, r[:c_len] * t[5], 0.0) for r, t in zip(rs, pre)]
    qks = [r[c_len:] * t[5] for r, t in zip(rs, pre)]
    tms = _tri_inv(lows, 6)
    egs = [jnp.exp(t[4]) for t in pre]
    uws = [_bdot(tm, jnp.concatenate([v * b, kb * eg], axis=1))
           for tm, eg, (q, k, v, b, gcol, decay, kb) in zip(tms, egs, pre)]

    for c in range(n_chunks):
        r0 = c * c_len
        idx = [c * GDN_HEADS + h for h in range(GDN_HEADS)]
        states = [s_ref[h] for h in range(GDN_HEADS)]
        wss = [_bdot(jnp.concatenate([uws[i][:, dh:], pre[i][0] * egs[i]], axis=0), s)
               for i, s in zip(idx, states)]
        v_news = [uws[i][:, :dh] - ws[:c_len] for i, ws in zip(idx, wss)]
        for h, (i, s, v_new) in enumerate(zip(idx, states, v_news)):
            k, gcol = pre[i][1], pre[i][4]
            g_last = gcol[c_len - 1:c_len, :]
            s_ref[h] = s * jnp.exp(g_last) + _bdot_tn(k * jnp.exp(g_last - gcol), v_new)
        for h, (i, ws, v_new) in enumerate(zip(idx, wss, v_news)):
            l0 = h * dh
            o = ws[c_len:] + _bdot(qks[i], v_new)
            zc = p_ref[r0:r0 + c_len, 3 * w + l0:3 * w + l0 + dh]
            o_ref[r0:r0 + c_len, l0:l0 + dh] = (_rms(o, nw_ref[...]) * _silu(zc)).astype(o_ref.dtype)


def gated_deltanet(p, conv_w, a_log, dt_bias, norm_w, batch, seq):
    t = p.shape[0]
    nt = seq // SEQ_TILE
    zeros = jnp.zeros((GDN_HEADS,), F32)
    tail = jnp.zeros((LANES - 2 * GDN_HEADS,), F32)
    bias = jnp.concatenate([zeros, dt_bias.astype(F32), tail])[None, :]
    a_row = jnp.concatenate([zeros, -jnp.exp(a_log.astype(F32)), tail])[None, :]
    params = (conv_w, bias, a_row, norm_w[None, :])
    return pl.pallas_call(
        _gdn_kernel,
        grid=(batch, nt),
        in_specs=[pl.BlockSpec((SEQ_TILE, GDN_P), lambda b, i: (b * nt + i, 0))]
                 + [_resident(v.shape) for v in params],
        out_specs=pl.BlockSpec((SEQ_TILE, GDN_WIDTH), lambda b, i: (b * nt + i, 0)),
        out_shape=jax.ShapeDtypeStruct((t, GDN_WIDTH), BF16),
        scratch_shapes=[pltpu.VMEM((PAD_ROWS + SEQ_TILE, 3 * GDN_WIDTH), F32),
                        pltpu.VMEM((GDN_HEADS, GDN_HEAD, GDN_HEAD), F32)],
        compiler_params=_cparams("parallel", "arbitrary"),
        name="gated_deltanet",
    )(p, *params)


def _gdn_weights(w_in_l):
    w = w_in_l[:, OFF_GDN:OFF_SSD]
    qkv = w[:, :3 * GDN_WIDTH]
    ba = w[:, 3 * GDN_WIDTH:3 * GDN_WIDTH + 2 * GDN_HEADS]
    z = w[:, 3 * GDN_WIDTH + 2 * GDN_HEADS:]
    return jnp.concatenate([qkv, z, ba, jnp.zeros((D_MODEL, LANES - 2 * GDN_HEADS), F32)],
                           axis=1).astype(BF16)


RWKV_XG_P = 2 * LANES


def _rwkv_kernel(p_ref, mu_ref, w0_ref, w2_ref, a0_ref, a2_ref, g2_ref, kk_ref, ka_ref, rk_ref,
                 lnw_ref, lnb_ref, o_ref, buf_ref, y_ref, s_ref):
    ts = SEQ_TILE
    c_len = CHUNK
    w = RWKV_WIDTH
    n = RWKV_HEAD

    @pl.when(pl.program_id(1) == 0)
    def _():
        buf_ref[0:PAD_ROWS, :] = jnp.zeros((PAD_ROWS, RWKV_P), F32)
        s_ref[...] = jnp.zeros(s_ref.shape, F32)

    cur = p_ref[...]
    buf_ref[PAD_ROWS:PAD_ROWS + ts, :] = cur
    prev = buf_ref[PAD_ROWS - 1:PAD_ROWS - 1 + ts, :]
    buf_ref[0:PAD_ROWS, :] = buf_ref[ts:ts + PAD_ROWS, :]
    p = cur + (prev - cur) * mu_ref[...]
    r = p[:, :w]
    k = p[:, w:2 * w]
    v = p[:, 2 * w:3 * w]
    lw = p[:, 3 * w:3 * w + LANES]
    xg = p[:, 3 * w + LANES:]

    sw = 2 * LANES
    seg_ones = jnp.where(_iota2((sw, sw), 0) // n == _iota2((sw, sw), 1) // n, 1.0, 0.0).astype(BF16)

    def seg_sum(t):
        return jnp.concatenate([_dot_split(t[:, j:j + sw], seg_ones) for j in range(0, w, sw)], axis=1)

    wl = w0_ref[...] + _bdot(jnp.tanh(lw), w2_ref[...])
    log_w = -jnp.exp(-_softplus(-wl) - 0.5)
    a = _sigmoid(a0_ref[...] + _bdot(lw, a2_ref[...]))
    gate = _bdot(_sigmoid(xg), g2_ref[...])
    kkr = k * kk_ref[...]
    kk = kkr * lax.rsqrt(seg_sum(kkr * kkr) + EPS)
    k_mod = k * (1.0 + (a - 1.0) * ka_ref[...])
    kka = kk * a

    cs = _split_dot(_chunk_tri(ts), log_w)
    p_inv = jnp.exp(-cs)
    rt = r * jnp.exp(cs)
    kkt = kk * jnp.exp(cs - log_w)
    kh = k_mod * p_inv
    kah = kka * p_inv

    lane = _iota2((1, LANES), 1)
    m0 = jnp.where(lane < n, 1.0, 0.0)
    m1 = 1.0 - m0
    row = _iota2((c_len, LANES), 0)
    col = _iota2((c_len, LANES), 1) % c_len
    strict2 = col < row
    incl2 = col <= row
    same_head = (_iota2((LANES, LANES), 0) // n) == (_iota2((LANES, LANES), 1) // n)
    halves = lambda t: jnp.concatenate([t * m0, t * m1], axis=0)

    n_chunks = ts // c_len
    n_pairs = RWKV_HEADS // 2
    chains = [(c, pr) for c in range(n_chunks) for pr in range(n_pairs)]
    pre = []
    for c, pr in chains:
        r0, l0 = c * c_len, pr * LANES
        sl = lambda t: t[r0:r0 + c_len, l0:l0 + LANES]
        p_last = jnp.exp(cs[r0 + c_len - 1:r0 + c_len, l0:l0 + LANES])
        pre.append((sl(kkt), sl(rt), sl(kh), sl(kah), sl(v), p_last))
    rrs = [_bdot_nt(jnp.concatenate([kkt_c, rt_c], axis=0),
                    jnp.concatenate([halves(kh_c), halves(kah_c)], axis=0))
           for kkt_c, rt_c, kh_c, kah_c, v_c, p_last in pre]
    lk2s = [jnp.where(strict2, rr[:c_len, :LANES], 0.0) for rr in rrs]
    la2s = [jnp.where(strict2, rr[:c_len, LANES:], 0.0) for rr in rrs]
    mq2s = [jnp.concatenate([jnp.where(incl2, rr[c_len:, :LANES], 0.0),
                             -jnp.where(incl2, rr[c_len:, LANES:], 0.0)], axis=1) for rr in rrs]
    vss = [halves(t[4]) for t in pre]
    lkvs = [_bdot(lk2, vs) for lk2, vs in zip(lk2s, vss)]
    tbds = _tri_inv([halves(la2) for la2 in la2s], 6)
    tws = [_bdot(tbd, jnp.concatenate([halves(t[0]), halves(lkv)], axis=1))
           for tbd, t, lkv in zip(tbds, pre, lkvs)]
    wu0 = [tw[:c_len] + tw[c_len:] for tw in tws]

    for c in range(n_chunks):
        r0 = c * c_len
        idx = [c * n_pairs + pr for pr in range(n_pairs)]
        states = [s_ref[pr] for pr in range(n_pairs)]
        xas = [_bdot_nt(jnp.concatenate([wu0[i][:, :LANES], pre[i][1]], axis=0), s)
               for i, s in zip(idx, states)]
        us = [xa[:c_len] + wu0[i][:, LANES:] for i, xa in zip(idx, xas)]
        for pr, (i, s, u) in enumerate(zip(idx, states, us)):
            kkt_c, rt_c, kh_c, kah_c, v_c, p_last = pre[i]
            upd = _bdot_tn(jnp.concatenate([v_c, u], axis=0),
                           jnp.concatenate([kh_c * p_last, -kah_c * p_last], axis=0))
            s_ref[pr] = s * p_last + jnp.where(same_head, upd, 0.0)
        for pr, (i, xa, u) in enumerate(zip(idx, xas, us)):
            l0 = pr * LANES
            y_ref[r0:r0 + c_len, l0:l0 + LANES] = xa[c_len:] + _bdot(
                mq2s[i], jnp.concatenate([vss[i], halves(u)], axis=0))

    o = y_ref[...]
    mean = seg_sum(o) * (1.0 / n)
    d = o - mean
    var = seg_sum(d * d) * (1.0 / n)
    o = d * lax.rsqrt(var + RWKV_GN_EPS) * lnw_ref[...] + lnb_ref[...]
    o = o + seg_sum(r * k_mod * rk_ref[...]) * v
    o_ref[...] = (o * gate).astype(o_ref.dtype)


def rwkv7_time_mix(p, mu, w0, w2, a0, a2, g2, k_k, k_a, r_k, ln_w, ln_b, batch, seq):
    t = p.shape[0]
    nt = seq // SEQ_TILE
    w3 = 3 * RWKV_WIDTH
    o4 = w3 + RWKV_DECAY_LORA
    o5 = o4 + RWKV_A_LORA
    row = lambda v: v.reshape(1, -1).astype(F32)
    mu_p = jnp.concatenate([mu, jnp.zeros((RWKV_XG_P - RWKV_GATE_LORA,), F32)])[None, :]
    zl = jnp.zeros((RWKV_DECAY_LORA, RWKV_WIDTH), F32)
    w2_p = jnp.concatenate([w2, zl], axis=0).astype(BF16)
    a2_p = jnp.concatenate([zl, a2], axis=0).astype(BF16)
    g2_p = jnp.concatenate([g2, jnp.zeros((RWKV_XG_P - RWKV_GATE_LORA, RWKV_WIDTH), F32)],
                           axis=0).astype(BF16)
    params = (mu_p, row(w0), w2_p, row(a0), a2_p, g2_p, row(k_k), row(k_a), row(r_k), row(ln_w), row(ln_b))
    return pl.pallas_call(
        _rwkv_kernel,
        grid=(batch, nt),
        in_specs=[pl.BlockSpec((SEQ_TILE, RWKV_P), lambda b, i: (b * nt + i, 0))]
                 + [_resident(v.shape) for v in params],
        out_specs=pl.BlockSpec((SEQ_TILE, RWKV_WIDTH), lambda b, i: (b * nt + i, 0)),
        out_shape=jax.ShapeDtypeStruct((t, RWKV_WIDTH), BF16),
        scratch_shapes=[pltpu.VMEM((PAD_ROWS + SEQ_TILE, RWKV_P), F32),
                        pltpu.VMEM((SEQ_TILE, RWKV_WIDTH), F32),
                        pltpu.VMEM((RWKV_HEADS // 2, LANES, LANES), F32)],
        compiler_params=_cparams("parallel", "arbitrary"),
        name="rwkv7_time_mix",
    )(p, *params)


def _rwkv_weights(w_in_l):
    w = w_in_l[:, OFF_RWKV:OFF_GDN]
    return jnp.concatenate([w, jnp.zeros((D_MODEL, RWKV_XG_P - RWKV_GATE_LORA), F32)], axis=1).astype(BF16)


MOE_FF_TILE = FF_EXPERT // 2
assert MOE_FF_TILE % MXU_COLS == 0


def _router_kernel(x_ref, g_ref, wr_ref, h_ref, idx_ref, wt_ref):
    h = _rms(x_ref[...], g_ref[...])
    h_ref[...] = h
    wr = wr_ref[...]
    h_hi = h.astype(BF16)
    h_lo = (h - h_hi.astype(F32)).astype(BF16)
    w_hi = wr.astype(BF16)
    w_lo = (wr - w_hi.astype(F32)).astype(BF16)
    logits = (jnp.dot(h_hi, w_hi, preferred_element_type=F32)
              + jnp.dot(h_lo, w_hi, preferred_element_type=F32)
              + jnp.dot(h_hi, w_lo, preferred_element_type=F32))
    lane = _iota2(logits.shape, 1)
    logits = jnp.where(lane < N_EXPERTS, logits, -jnp.inf)
    m1 = jnp.max(logits, axis=-1, keepdims=True)
    i1 = jnp.min(jnp.where(logits == m1, lane, LANES), axis=-1, keepdims=True)
    rest = jnp.where(lane == i1, -jnp.inf, logits)
    m2 = jnp.max(rest, axis=-1, keepdims=True)
    i2 = jnp.min(jnp.where(rest == m2, lane, LANES), axis=-1, keepdims=True)
    e = jnp.exp(m2 - m1)
    w1 = 1.0 / (1.0 + e)
    idx_ref[...] = jnp.where(lane == 0, i1, i2)
    wt_ref[...] = jnp.where(lane == 0, w1, e * w1)


def moe_route(x, g, router):
    t = x.shape[0]
    wr = jnp.concatenate([router.astype(F32), jnp.zeros((D_MODEL, LANES - N_EXPERTS), F32)], axis=1)
    return pl.pallas_call(
        _router_kernel,
        grid=(t // ROW_TILE,),
        in_specs=[_rows(ROW_TILE, D_MODEL), _resident((1, D_MODEL)), _resident(wr.shape)],
        out_specs=[_rows(ROW_TILE, D_MODEL), _rows(ROW_TILE, LANES), _rows(ROW_TILE, LANES)],
        out_shape=[jax.ShapeDtypeStruct((t, D_MODEL), F32), jax.ShapeDtypeStruct((t, LANES), jnp.int32),
                   jax.ShapeDtypeStruct((t, LANES), F32)],
        compiler_params=_cparams("parallel"),
        name="moe_router",
    )(x, g, wr)


DMA_UNROLL = 8


def _moe_plan(idx):
    t = idx.shape[0]
    n_pairs = t * TOP_K
    n_blk = -(-(n_pairs + N_EXPERTS * (MOE_BLOCK - 1)) // MOE_BLOCK)
    n_slot = n_blk * MOE_BLOCK
    flat_e = idx[:, :TOP_K].reshape(n_pairs)
    onehot = (flat_e[:, None] == jnp.arange(N_EXPERTS, dtype=jnp.int32)[None, :]).astype(jnp.int32)
    rank = jnp.sum((jnp.cumsum(onehot, axis=0) - onehot) * onehot, axis=1)
    counts = jnp.sum(onehot, axis=0)
    padded = (counts + MOE_BLOCK - 1) // MOE_BLOCK * MOE_BLOCK
    pad_end = jnp.cumsum(padded)
    pad_start = pad_end - padded
    dest = pad_start[flat_e] + rank
    pair_id = jnp.arange(n_pairs, dtype=jnp.int32)
    pair_row = (pair_id % TOP_K) * t + pair_id // TOP_K
    spare = n_pairs + jnp.arange(n_slot, dtype=jnp.int32) % MOE_BLOCK
    slot_dst = spare.at[dest].set(pair_row)
    slot_tok = jnp.where(slot_dst < n_pairs, slot_dst % t, 0)
    blk_start = jnp.arange(n_blk, dtype=jnp.int32) * MOE_BLOCK
    blk_e = jnp.minimum(jnp.sum((pad_end[None, :] <= blk_start[:, None]).astype(jnp.int32), axis=1),
                        N_EXPERTS - 1)
    return slot_tok, slot_dst, blk_e


def _moe_kernel(blk_e_ref, tok_ref, tokn_ref, dst_ref, dstp_ref, h_hbm, wg_ref, wu_ref, wd_ref, out_hbm,
                xbuf, xb_ref, acc_ref, ybuf, gsem, ssem):
    n = pl.program_id(0)
    f = pl.program_id(1)
    n_blk = pl.num_programs(0)
    n_ff = FF_EXPERT // MOE_FF_TILE
    rows_per_step = MOE_BLOCK // n_ff
    slot = n % 2

    def row_in(tok, q, j, s):
        return pltpu.make_async_copy(h_hbm.at[pl.ds(tok, 1), :], xbuf.at[s, q, pl.ds(j, 1), :], gsem.at[s])

    def row_out(q, j, d):
        return pltpu.make_async_copy(ybuf.at[q, pl.ds(j, 1), :], out_hbm.at[pl.ds(d, 1), :], ssem.at[0])

    def for_rows(fn):
        def body(r, carry):
            fn(r, r // rows_per_step, r % rows_per_step)
            return carry
        lax.fori_loop(0, MOE_BLOCK, body, 0, unroll=DMA_UNROLL)

    @pl.when(f == 0)
    def _():
        @pl.when(n == 0)
        def _():
            for_rows(lambda r, q, j: row_in(tok_ref[0, 0, r], q, j, 0).start())
            ybuf[...] = jnp.zeros(ybuf.shape, F32)

        for_rows(lambda r, q, j: row_in(tok_ref[0, 0, r], q, j, slot).wait())
        xb_ref[...] = xbuf[slot].reshape(MOE_BLOCK, D_MODEL).astype(BF16)

    r0 = f * rows_per_step
    for j in range(rows_per_step):
        row_in(tokn_ref[0, 0, r0 + j], f, j, 1 - slot).start()
        row_out(f, j, dstp_ref[0, 0, r0 + j]).start()

    xb = xb_ref[...]
    gate = jnp.dot(xb, wg_ref[0], preferred_element_type=F32)
    up = jnp.dot(xb, wu_ref[0], preferred_element_type=F32)
    part = jnp.dot((_silu(gate) * up).astype(BF16), wd_ref[0], preferred_element_type=F32)

    @pl.when(f == 0)
    def _():
        acc_ref[...] = part

    @pl.when(jnp.logical_and(f > 0, f < n_ff - 1))
    def _():
        acc_ref[...] += part

    @pl.when(f == n_ff - 1)
    def _():
        for_rows(lambda r, q, j: row_out(q, j, dstp_ref[0, 0, r]).wait())
        ybuf[...] = (acc_ref[...] + part).reshape(ybuf.shape)

        @pl.when(n == n_blk - 1)
        def _():
            for_rows(lambda r, q, j: row_out(q, j, dst_ref[0, 0, r]).start())
            for_rows(lambda r, q, j: row_out(q, j, dst_ref[0, 0, r]).wait())
            for_rows(lambda r, q, j: row_in(tokn_ref[0, 0, r], q, j, 1 - slot).wait())


def moe_experts(h, slot_tok, slot_dst, blk_e, w_gu, w_down):
    t = h.shape[0]
    n_blk = blk_e.shape[0]
    n_ff = FF_EXPERT // MOE_FF_TILE
    tok3 = slot_tok.reshape(n_blk, 1, MOE_BLOCK)
    dst3 = slot_dst.reshape(n_blk, 1, MOE_BLOCK)
    smem_blk = lambda imap: pl.BlockSpec((1, 1, MOE_BLOCK), imap, memory_space=pltpu.SMEM)
    grid_spec = pltpu.PrefetchScalarGridSpec(
        num_scalar_prefetch=1,
        grid=(n_blk, n_ff),
        in_specs=[
            smem_blk(lambda n, f, be: (n, 0, 0)),
            smem_blk(lambda n, f, be: (jnp.minimum(n + 1, be.shape[0] - 1), 0, 0)),
            smem_blk(lambda n, f, be: (n, 0, 0)),
            smem_blk(lambda n, f, be: (jnp.maximum(n - 1, 0), 0, 0)),
            pl.BlockSpec(memory_space=pl.ANY),
            pl.BlockSpec((1, D_MODEL, MOE_FF_TILE), lambda n, f, be: (be[n], 0, f)),
            pl.BlockSpec((1, D_MODEL, MOE_FF_TILE), lambda n, f, be: (be[n], 0, f + FF_EXPERT // MOE_FF_TILE)),
            pl.BlockSpec((1, MOE_FF_TILE, D_MODEL), lambda n, f, be: (be[n], f, 0)),
        ],
        out_specs=pl.BlockSpec(memory_space=pl.ANY),
        scratch_shapes=[pltpu.VMEM((2, n_ff, MOE_BLOCK // n_ff, D_MODEL), F32),
                        pltpu.VMEM((MOE_BLOCK, D_MODEL), BF16),
                        pltpu.VMEM((MOE_BLOCK, D_MODEL), F32),
                        pltpu.VMEM((n_ff, MOE_BLOCK // n_ff, D_MODEL), F32),
                        pltpu.SemaphoreType.DMA((2,)),
                        pltpu.SemaphoreType.DMA((1,))],
    )
    return pl.pallas_call(
        _moe_kernel,
        grid_spec=grid_spec,
        out_shape=jax.ShapeDtypeStruct((TOP_K * t + MOE_BLOCK, D_MODEL), F32),
        compiler_params=_cparams("arbitrary", "arbitrary"),
        name="moe_experts",
    )(blk_e, tok3, tok3, dst3, dst3, h, w_gu, w_gu, w_down)


def _moe_combine_kernel(x_ref, w_ref, y0_ref, y1_ref, *rest):
    w = w_ref[...]
    y = x_ref[...] + w[:, 0:1] * y0_ref[...] + w[:, 1:2] * y1_ref[...]
    if len(rest) == 2:
        g_ref, o_ref = rest
        o_ref[...] = _rms(y, g_ref[...])
    else:
        rest[0][...] = y


def moe_combine(x, wts, pair_out, out_norm=None):
    t = x.shape[0]
    nt = t // ROW_TILE
    extra = () if out_norm is None else (out_norm,)
    return pl.pallas_call(
        _moe_combine_kernel,
        grid=(nt,),
        in_specs=[_rows(ROW_TILE, D_MODEL), _rows(ROW_TILE, LANES), _rows(ROW_TILE, D_MODEL),
                  pl.BlockSpec((ROW_TILE, D_MODEL), lambda i: (i + nt, 0))]
                 + [_resident(v.shape) for v in extra],
        out_specs=_rows(ROW_TILE, D_MODEL),
        out_shape=jax.ShapeDtypeStruct((t, D_MODEL), F32),
        compiler_params=_cparams("parallel"),
        name="moe_combine",
    )(x, wts, pair_out, pair_out, *extra)


def moe_swiglu(x, g, router, w_gu, w_down, out_norm=None):
    h, idx, wts = moe_route(x, g, router)
    slot_tok, slot_dst, blk_e = _moe_plan(idx)
    pair_out = moe_experts(h, slot_tok, slot_dst, blk_e, w_gu, w_down)
    return moe_combine(x, wts, pair_out, out_norm)


def kernel(x, positions, norm_mix, w_in, mla_q_norm, mla_kv_norm, mla_w_uq, mla_w_ukv, rwkv_mu, rwkv_w0,
           rwkv_w2, rwkv_a0, rwkv_a2, rwkv_g2, rwkv_k_k, rwkv_k_a, rwkv_r_k, rwkv_ln_w, rwkv_ln_b, gdn_conv,
           gdn_a_log, gdn_dt_bias, gdn_norm, ssd_conv_w, ssd_conv_b, ssd_dt_bias, ssd_a_log, ssd_d, ssd_norm,
           gate_b, w_branch, w_out, norm_ffn, ffn_w_gu, ffn_w_down, moe_router, moe_w_gu, moe_w_down,
           norm_final):
    batch, seq, d = x.shape
    t = batch * seq
    depth = w_in.shape[0]
    xf = x.reshape(t, d)
    cos, sin = rope_tables(positions.reshape(t, 1).astype(jnp.int32))
    row = lambda v: v.reshape(1, -1)
    for layer in range(depth):
        w_l = w_in[layer]
        w_a, wq, wqr, wk, wv = _mla_weights(w_l, mla_w_uq[layer], mla_w_ukv[layer])
        pa, pb, pc, pd = norm_proj(xf, row(norm_mix[layer]),
                                   [w_a, _rwkv_weights(w_l), _gdn_weights(w_l), _ssd_weights(w_l)])
        y_a = mla_attention(pa, cos, sin, row(mla_q_norm[layer]), row(mla_kv_norm[layer]),
                            wq, wqr, wk, wv, batch, seq)
        y_b = rwkv7_time_mix(pb, rwkv_mu[layer], rwkv_w0[layer], rwkv_w2[layer], rwkv_a0[layer],
                             rwkv_a2[layer], rwkv_g2[layer], rwkv_k_k[layer], rwkv_k_a[layer],
                             rwkv_r_k[layer], rwkv_ln_w[layer], rwkv_ln_b[layer], batch, seq)
        y_c = gated_deltanet(pc, gdn_conv[layer], gdn_a_log[layer], gdn_dt_bias[layer], gdn_norm[layer],
                             batch, seq)
        y_d = mamba2_ssd(pd, ssd_conv_w[layer], ssd_conv_b[layer], ssd_dt_bias[layer], ssd_a_log[layer],
                         ssd_d[layer], ssd_norm[layer], batch, seq)
        w_gate = w_l[:, OFF_GATE:].reshape(D_MODEL, N_BRANCH, D_MODEL).transpose(1, 0, 2).astype(BF16)
        xf = merge(xf, row(norm_mix[layer]), w_gate, gate_b[layer], (y_a, y_b, y_c, y_d),
                   w_branch[layer].astype(BF16), w_out[layer].astype(BF16))
        last = layer == depth - 1
        if layer % 2 == 0:
            xf = ffn_dense(xf, row(norm_ffn[layer]), ffn_w_gu[layer // 2].astype(BF16),
                           ffn_w_down[layer // 2].astype(BF16))
            if last:
                xf = final_norm(xf, row(norm_final))
        else:
            xf = moe_swiglu(xf, row(norm_ffn[layer]), moe_router[layer // 2],
                            moe_w_gu[layer // 2].astype(BF16), moe_w_down[layer // 2].astype(BF16),
                            out_norm=row(norm_final) if last else None)
    return xf.reshape(batch, seq, d)
```
